```python
import math
import jax
import jax.numpy as jnp
from jax import lax
import numpy as np

D_MODEL = 2048
BATCH = 8
SEQ = 4096
DEPTH = 2
DEC_BATCH = 1
DEC_SEQ = 8192
PAST_LEN = 128

N_META = 16
CHUNK = 64
PAD = CHUNK - N_META
WIDTH = D_MODEL // 2
N_BRANCH = 4
RMS_EPS = 1e-6
NEG = -1e30

MLSTM_HEADS = 4
MLSTM_DV = WIDTH // MLSTM_HEADS
MLSTM_DQK = MLSTM_DV // 2

HGRN_EXPAND = 128
HGRN_HEADS = WIDTH // HGRN_EXPAND

RWKV_HEAD = 64
RWKV_HEADS = WIDTH // RWKV_HEAD
RWKV_LORA_W = 64
RWKV_LORA_A = 64
RWKV_GN_EPS = 64e-5
RWKV_SHIFT_COLS = 3 * WIDTH + 2 * RWKV_LORA_W + 2 * RWKV_LORA_A

S5_GROUP = 16
S5_GROUPS = WIDTH // S5_GROUP
S5_STATE = 64

COL_SIZES = (
    MLSTM_HEADS * MLSTM_DQK,
    MLSTM_HEADS * MLSTM_DQK,
    WIDTH,
    WIDTH,
    2 * MLSTM_HEADS,
    2 * MLSTM_HEADS,
    WIDTH,
    WIDTH,
    2 * WIDTH,
    WIDTH,
    WIDTH,
    WIDTH,
    WIDTH,
    WIDTH,
    2 * RWKV_LORA_W,
    2 * RWKV_LORA_A,
    WIDTH,
    WIDTH,
    WIDTH,
)
N_COLS = sum(COL_SIZES)

kernel_name = 'hybrid_bidir_mlstm_hgrn2_rwkv7_s5_encoder'


def _rmsnorm(x, w):
    xf = x.astype(jnp.float32)
    y = xf * lax.rsqrt(jnp.mean(xf * xf, axis=-1, keepdims=True) + RMS_EPS)
    return (y * w.astype(jnp.float32)).astype(x.dtype)


def _head_rms(h, w):
    return h * lax.rsqrt(jnp.mean(h * h, axis=-1, keepdims=True) + RMS_EPS) * w


def _split_cols(z):
    offs = []
    acc = 0
    for s in COL_SIZES[:-1]:
        acc += s
        offs.append(acc)
    return jnp.split(z, offs, axis=-1)


def _pad_front(x, value):
    cfg = [(0, 0)] * x.ndim
    cfg[1] = (PAD, 0)
    return jnp.pad(x, cfg, constant_values=value)


def _bidir(xf, xb, pad_value):
    if pad_value is not None:
        xf = _pad_front(xf, pad_value)
        xb = _pad_front(xb, pad_value)
    return jnp.concatenate([xf, jnp.flip(xb, axis=1)], axis=0)


def _merge_dirs(y, n, padded):
    out = y[:n] + jnp.flip(y[n:], axis=1)
    return out[:, PAD:] if padded else out


def _to_chunks(x):
    n, tp = x.shape[:2]
    x = x.reshape((n, tp // CHUNK, CHUNK) + x.shape[2:])
    return jnp.swapaxes(jnp.moveaxis(x, 1, 0), 2, 3)


def _from_chunks(y):
    nc, n, h, l, d = y.shape
    return jnp.moveaxis(jnp.swapaxes(y, 2, 3), 0, 1).reshape(n, nc * l, h, d)


def _mlstm_chunked(q, k, v, log_i, log_f):
    n2 = q.shape[0]
    mask = jnp.tril(jnp.ones((CHUNK, CHUNK), dtype=bool))

    def step(carry, inp):
        c_st, n_st, m_st = carry
        q_, k_, v_, li, lf = inp
        b = jnp.cumsum(lf, axis=-1)
        dlog = jnp.where(mask, b[..., :, None] - b[..., None, :] + li[..., None, :], NEG)
        a = b + m_st[..., None]
        mt = jnp.maximum(a, jnp.max(dlog, axis=-1))
        w = jnp.exp(dlog - mt[..., None]) * jnp.einsum('nhtd,nhsd->nhts', q_, k_)
        inter = jnp.exp(a - mt)
        num = inter[..., None] * jnp.einsum('nhtd,nhde->nhte', q_, c_st) + jnp.einsum('nhts,nhse->nhte', w, v_)
        den = inter * jnp.einsum('nhtd,nhd->nht', q_, n_st) + jnp.sum(w, axis=-1)
        h = num / jnp.maximum(jnp.abs(den), jnp.exp(-mt))[..., None]
        bl = b[..., -1]
        tail = bl[..., None] - b + li
        m_new = jnp.maximum(bl + m_st, jnp.max(tail, axis=-1))
        ws = jnp.exp(tail - m_new[..., None])
        decay = jnp.exp(bl + m_st - m_new)
        c_st = decay[..., None, None] * c_st + jnp.einsum('nhs,nhsd,nhse->nhde', ws, k_, v_)
        n_st = decay[..., None] * n_st + jnp.einsum('nhs,nhsd->nhd', ws, k_)
        return (c_st, n_st, m_new), h

    init = (jnp.zeros((n2, MLSTM_HEADS, MLSTM_DQK, MLSTM_DV), jnp.float32),
            jnp.zeros((n2, MLSTM_HEADS, MLSTM_DQK), jnp.float32),
            jnp.zeros((n2, MLSTM_HEADS), jnp.float32))
    xs = (_to_chunks(q), _to_chunks(k), _to_chunks(v), _to_chunks(log_i), _to_chunks(log_f))
    _, h = lax.scan(step, init, xs)
    return _from_chunks(h)


def _hgrn2_chunked(q, k, v, log_f):
    n2 = q.shape[0]
    mask = jnp.tril(jnp.ones((CHUNK, CHUNK), dtype=bool))

    def step(s_st, inp):
        q_, k_, v_, g = inp
        b = jnp.cumsum(g, axis=2)
        diff = b[:, :, :, None, :] - b[:, :, None, :, :]
        e = jnp.exp(jnp.where(mask[:, :, None], diff, NEG))
        attn = jnp.einsum('nhtd,nhsd,nhtsd->nhts', q_, k_, e)
        o = jnp.einsum('nhtd,nhde->nhte', q_ * jnp.exp(b), s_st) + jnp.einsum('nhts,nhse->nhte', attn, v_)
        bl = b[:, :, -1]
        s_st = jnp.exp(bl)[..., None] * s_st + jnp.einsum('nhsd,nhse->nhde', k_ * jnp.exp(bl[:, :, None] - b), v_)
        return s_st, o

    init = jnp.zeros((n2, HGRN_HEADS, HGRN_EXPAND, HGRN_EXPAND), jnp.float32)
    xs = (_to_chunks(q), _to_chunks(k), _to_chunks(v), _to_chunks(log_f))
    _, o = lax.scan(step, init, xs)
    return _from_chunks(o)


def _rwkv7_scan(r, log_w, k, v, kk, a):
    n2 = r.shape[0]
    xs = tuple(jnp.moveaxis(t, 1, 0) for t in (r, log_w, k, v, kk, a))

    def step(s_st, inp):
        r_, lw_, k_, v_, kk_, a_ = inp
        sa = jnp.einsum('nhvk,nhk->nhv', s_st, -kk_)
        s_st = (s_st * jnp.exp(lw_)[:, :, None, :] + sa[..., None] * (kk_ * a_)[:, :, None, :]
                + v_[..., None] * k_[:, :, None, :])
        return s_st, jnp.einsum('nhvk,nhk->nhv', s_st, r_)

    init = jnp.zeros((n2, RWKV_HEADS, RWKV_HEAD, RWKV_HEAD), jnp.float32)
    _, y = lax.scan(step, init, xs)
    return jnp.moveaxis(y, 0, 1)


def _ssm_combine(e1, e2):
    a1r, a1i, b1r, b1i = e1
    a2r, a2i, b2r, b2i = e2
    return (a2r * a1r - a2i * a1i, a2r * a1i + a2i * a1r,
            a2r * b1r - a2i * b1i + b2r, a2r * b1i + a2i * b1r + b2i)


def _s5_scan(u, a_re, a_im, log_dt, b_re, b_im, c_re, c_im):
    n = u.shape[0]
    a_re, a_im, log_dt = a_re.astype(jnp.float32), a_im.astype(jnp.float32), log_dt.astype(jnp.float32)
    b_re, b_im = b_re.astype(jnp.float32), b_im.astype(jnp.float32)
    c_re, c_im = c_re.astype(jnp.float32), c_im.astype(jnp.float32)
    up = _pad_front(u, 0.0).reshape(n, -1, S5_GROUPS, S5_GROUP)
    tp = up.shape[1]
    nc = tp // CHUNK
    u2 = jnp.stack([up, jnp.flip(up, axis=1)])
    uc = jnp.moveaxis(u2.reshape(2, n, nc, CHUNK, S5_GROUPS, S5_GROUP), 2, 0)
    dt = jnp.exp(log_dt)[..., None]
    mag = jnp.exp(a_re * dt)
    ang = a_im * dt
    abar_re, abar_im = mag * jnp.cos(ang), mag * jnp.sin(ang)
    den = a_re * a_re + a_im * a_im
    xr_, yi_ = abar_re - 1.0, abar_im
    coef_re = (xr_ * a_re + yi_ * a_im) / den
    coef_im = (yi_ * a_re - xr_ * a_im) / den
    bbar_re = coef_re[..., None] * b_re - coef_im[..., None] * b_im
    bbar_im = coef_re[..., None] * b_im + coef_im[..., None] * b_re

    def step(carry, u_c):
        cr, ci = carry
        bur = jnp.einsum('dnlgc,dgpc->dnlgp', u_c, bbar_re)
        bui = jnp.einsum('dnlgc,dgpc->dnlgp', u_c, bbar_im)
        ar = jnp.broadcast_to(abar_re[:, None, None], bur.shape)
        ai = jnp.broadcast_to(abar_im[:, None, None], bur.shape)
        pr, pim, xr, xi = lax.associative_scan(_ssm_combine, (ar, ai, bur, bui), axis=2)
        xr = xr + pr * cr[:, :, None] - pim * ci[:, :, None]
        xi = xi + pr * ci[:, :, None] + pim * cr[:, :, None]
        y = jnp.einsum('dnlgp,dgcp->dnlgc', xr, c_re) - jnp.einsum('dnlgp,dgcp->dnlgc', xi, c_im)
        return (xr[:, :, -1], xi[:, :, -1]), y

    init = (jnp.zeros((2, n, S5_GROUPS, S5_STATE), jnp.float32),
            jnp.zeros((2, n, S5_GROUPS, S5_STATE), jnp.float32))
    _, y = lax.scan(step, init, uc)
    y = jnp.moveaxis(y, 0, 2).reshape(2, n, tp, WIDTH)
    y = y[0] + jnp.flip(y[1], axis=1)
    return y[:, PAD:]


def _branch_mlstm(qa, ka, va, oa, iga, fga, ga, ig_b, fg_b, norm_w):
    n, t, _ = qa.shape
    q = qa.reshape(n, t, MLSTM_HEADS, MLSTM_DQK)
    k = ka.reshape(n, t, MLSTM_HEADS, MLSTM_DQK) * (MLSTM_DQK ** -0.5)
    v = va.reshape(n, t, MLSTM_HEADS, MLSTM_DV)
    log_i = iga.reshape(n, t, 2, MLSTM_HEADS) + ig_b
    log_f = jax.nn.log_sigmoid(fga.reshape(n, t, 2, MLSTM_HEADS) + fg_b)
    h = _mlstm_chunked(_bidir(q, q, 0.0), _bidir(k, k, 0.0), _bidir(v, v, 0.0),
                       _bidir(log_i[:, :, 0], log_i[:, :, 1], NEG),
                       _bidir(log_f[:, :, 0], log_f[:, :, 1], 0.0))
    h = _head_rms(_merge_dirs(h, n, True), norm_w.reshape(MLSTM_HEADS, MLSTM_DV)).reshape(n, t, WIDTH)
    return h * jax.nn.sigmoid(oa) * jax.nn.silu(ga)


def _branch_hgrn(qb, fb, ib, gb, lb, norm_w):
    n, t, _ = qb.shape
    heads = lambda z: z.reshape(z.shape[:2] + (HGRN_HEADS, HGRN_EXPAND))
    q = heads(jax.nn.silu(qb))
    v = heads(ib)
    fpre = fb.reshape(n, t, 2, WIDTH)
    log_f = jnp.log(lb + (1.0 - lb) * jax.nn.sigmoid(fpre))
    k = (1.0 - lb) * jax.nn.sigmoid(-fpre)
    o = _hgrn2_chunked(_bidir(q, q, 0.0), _bidir(heads(k[:, :, 0]), heads(k[:, :, 1]), 0.0),
                       _bidir(v, v, 0.0), _bidir(heads(log_f[:, :, 0]), heads(log_f[:, :, 1]), 0.0))
    o = _head_rms(_merge_dirs(o, n, True), norm_w).reshape(n, t, WIDTH)
    return o * jax.nn.silu(gb)


def _branch_rwkv(rc, kc, vc, wlc, alc, gc, mu, w0, w2, a0, a2, k_k, k_a, r_k, ln_w, ln_b):
    n, t, _ = rc.shape
    f = jnp.concatenate([rc, kc, vc, wlc, alc], axis=-1)
    prev = jnp.pad(f, ((0, 0), (1, 0), (0, 0)))[:, :-1]
    nxt = jnp.pad(f, ((0, 0), (0, 1), (0, 0)))[:, 1:]
    f = f + mu[0] * (prev - f) + mu[1] * (nxt - f)
    r, k, v, wl, al = jnp.split(f, [WIDTH, 2 * WIDTH, 3 * WIDTH, 3 * WIDTH + 2 * RWKV_LORA_W], axis=-1)
    w = w0 + jnp.einsum('ntdr,drc->ntdc', jnp.tanh(wl.reshape(n, t, 2, RWKV_LORA_W)), w2)
    log_w = -jnp.exp(-jax.nn.softplus(-w) - 0.5)
    a = jax.nn.sigmoid(a0 + jnp.einsum('ntdr,drc->ntdc', al.reshape(n, t, 2, RWKV_LORA_A), a2))
    heads = lambda z: z.reshape(z.shape[:2] + (RWKV_HEADS, RWKV_HEAD))
    kk = heads(k * k_k)
    kk = kk / jnp.maximum(jnp.sqrt(jnp.sum(kk * kk, axis=-1, keepdims=True)), 1e-12)
    kd = k[:, :, None] * (1.0 + (a - 1.0) * k_a)
    r_h, v_h = heads(r), heads(v)
    kf, kb = heads(kd[:, :, 0]), heads(kd[:, :, 1])
    y = _rwkv7_scan(_bidir(r_h, r_h, None), _bidir(heads(log_w[:, :, 0]), heads(log_w[:, :, 1]), None),
                    _bidir(kf, kb, None), _bidir(v_h, v_h, None), _bidir(kk, kk, None),
                    _bidir(heads(a[:, :, 0]), heads(a[:, :, 1]), None))
    y = _merge_dirs(y, n, False)
    mean = jnp.mean(y, axis=-1, keepdims=True)
    var = jnp.mean((y - mean) ** 2, axis=-1, keepdims=True)
    y = (y - mean) * lax.rsqrt(var + RWKV_GN_EPS) * ln_w.reshape(RWKV_HEADS, RWKV_HEAD) + ln_b.reshape(RWKV_HEADS, RWKV_HEAD)
    bonus = jnp.sum(r_h * (kf + kb) * r_k, axis=-1, keepdims=True) * v_h
    return (y + bonus).reshape(n, t, WIDTH) * jax.nn.silu(gc)


def _branch_s5(ud, gd, a_re, a_im, log_dt, b_re, b_im, c_re, c_im, d, glu_w, glu_b):
    y = _s5_scan(ud, a_re, a_im, log_dt, b_re, b_im, c_re, c_im) + d * ud
    g = jax.nn.gelu(y)
    y = g * jax.nn.sigmoid(jnp.einsum('ntw,wc->ntc', g, glu_w.astype(jnp.float32)) + glu_b)
    return y * jax.nn.silu(gd)


def _layer(x, l, p):
    h = _rmsnorm(x, p['norm_w'][l])
    z = jnp.einsum('ntd,dc->ntc', h, p['w_in'][l]).astype(jnp.float32)
    (qa, ka, va, oa, iga, fga, ga, qb, fb, ib, gb, rc, kc, vc, wlc, alc, gc, ud, gd) = _split_cols(z)
    y_a = _branch_mlstm(qa, ka, va, oa, iga, fga, ga, p['mlstm_ig_b'][l], p['mlstm_fg_b'][l], p['mlstm_norm_w'][l])
    y_b = _branch_hgrn(qb, fb, ib, gb, p['hgrn_lb'][:, l], p['hgrn_norm_w'][l])
    y_c = _branch_rwkv(rc, kc, vc, wlc, alc, gc, p['rwkv_shift_mu'][l], p['rwkv_w0'][l], p['rwkv_w2'][l],
                       p['rwkv_a0'][l], p['rwkv_a2'][l], p['rwkv_k_k'][l], p['rwkv_k_a'][l], p['rwkv_r_k'][l],
                       p['rwkv_ln_w'][l], p['rwkv_ln_b'][l])
    y_d = _branch_s5(ud, gd, p['s5_a_re'][l], p['s5_a_im'][l], p['s5_log_dt'][l], p['s5_b_re'][l], p['s5_b_im'][l],
                     p['s5_c_re'][l], p['s5_c_im'][l], p['s5_d'][l], p['s5_glu_w'][l], p['s5_glu_b'][l])
    merged = None
    for b, y in enumerate((y_a, y_b, y_c, y_d)):
        proj = jnp.einsum('ntw,wd->ntd', y.astype(x.dtype), p['w_branch'][l, b])
        gate = jax.nn.sigmoid(jnp.einsum('ntd,de->nte', h, p['w_gate'][l, b]) + p['b_gate'][l, b])
        term = gate * proj
        merged = term if b == 0 else merged + term
    return x + jnp.einsum('ntd,de->nte', merged, p['w_out'][l]).astype(x.dtype)


def _trunk(x, p):
    n = x.shape[0]
    meta = jnp.broadcast_to(p['meta_tokens'].astype(x.dtype)[None], (n, N_META, D_MODEL))
    h = jnp.concatenate([meta, x], axis=1)
    for l in range(DEPTH):
        h = _layer(h, l, p)
    h = _rmsnorm(h, p['final_norm_w'])
    return h[:, N_META:]


def setup_inputs(seed: int = 0) -> dict:
    key = jax.random.key(seed)
    ks = jax.random.split(key, 40)
    nrm = lambda i, shape, s=1.0: s * jax.random.normal(ks[i], shape, jnp.float32)
    a_im = math.pi * jnp.arange(S5_STATE, dtype=jnp.float32) + nrm(22, (DEPTH, 2, S5_GROUPS, S5_STATE), 0.01)
    return {
        'x_prompt': nrm(0, (BATCH, SEQ, D_MODEL)),
        'x_sample': nrm(1, (DEC_BATCH, DEC_SEQ, D_MODEL)),
        'meta_tokens': nrm(2, (N_META, D_MODEL)),
        'norm_w': 1.0 + nrm(3, (DEPTH, D_MODEL), 0.01),
        'w_in': nrm(4, (DEPTH, D_MODEL, N_COLS), D_MODEL ** -0.5),
        'mlstm_ig_b': nrm(5, (DEPTH, 2, MLSTM_HEADS), 0.1),
        'mlstm_fg_b': jnp.linspace(3.0, 6.0, MLSTM_HEADS, dtype=jnp.float32) + nrm(6, (DEPTH, 2, MLSTM_HEADS), 0.1),
        'mlstm_norm_w': 1.0 + nrm(7, (DEPTH, WIDTH), 0.01),
        'hgrn_lower_bounds': nrm(8, (2, DEPTH, WIDTH), 0.1),
        'hgrn_norm_w': 1.0 + nrm(9, (DEPTH, HGRN_EXPAND), 0.01),
        'rwkv_shift_mu': jax.random.uniform(ks[10], (DEPTH, 2, RWKV_SHIFT_COLS), jnp.float32, 0.0, 0.5),
        'rwkv_w0': nrm(11, (DEPTH, 2, WIDTH), 0.5),
        'rwkv_w2': nrm(12, (DEPTH, 2, RWKV_LORA_W, WIDTH), RWKV_LORA_W ** -0.5),
        'rwkv_a0': nrm(13, (DEPTH, 2, WIDTH), 0.1),
        'rwkv_a2': nrm(14, (DEPTH, 2, RWKV_LORA_A, WIDTH), RWKV_LORA_A ** -0.5),
        'rwkv_k_k': 0.85 + nrm(15, (DEPTH, WIDTH), 0.05),
        'rwkv_k_a': 1.0 + nrm(16, (DEPTH, WIDTH), 0.05),
        'rwkv_r_k': nrm(17, (DEPTH, RWKV_HEADS, RWKV_HEAD), 0.1),
        'rwkv_ln_w': 1.0 + nrm(18, (DEPTH, WIDTH), 0.01),
        'rwkv_ln_b': nrm(19, (DEPTH, WIDTH), 0.01),
        's5_a_re': -0.5 + nrm(20, (DEPTH, 2, S5_GROUPS, S5_STATE), 0.01),
        's5_a_im': a_im,
        's5_log_dt': jax.random.uniform(ks[21], (DEPTH, 2, S5_GROUPS), jnp.float32, math.log(1e-3), math.log(1e-1)),
        's5_b_re': nrm(23, (DEPTH, 2, S5_GROUPS, S5_STATE, S5_GROUP), (2 * S5_GROUP) ** -0.5),
        's5_b_im': nrm(24, (DEPTH, 2, S5_GROUPS, S5_STATE, S5_GROUP), (2 * S5_GROUP) ** -0.5),
        's5_c_re': nrm(25, (DEPTH, 2, S5_GROUPS, S5_GROUP, S5_STATE), S5_STATE ** -0.5),
        's5_c_im': nrm(26, (DEPTH, 2, S5_GROUPS, S5_GROUP, S5_STATE), S5_STATE ** -0.5),
        's5_d': nrm(27, (DEPTH, WIDTH)),
        's5_glu_w': nrm(28, (DEPTH, WIDTH, WIDTH), WIDTH ** -0.5),
        's5_glu_b': nrm(29, (DEPTH, WIDTH), 0.01),
        'w_branch': nrm(30, (DEPTH, N_BRANCH, WIDTH, D_MODEL), WIDTH ** -0.5),
        'w_gate': nrm(31, (DEPTH, N_BRANCH, D_MODEL, D_MODEL), D_MODEL ** -0.5),
        'b_gate': nrm(32, (DEPTH, N_BRANCH, D_MODEL), 0.01),
        'w_out': nrm(33, (DEPTH, D_MODEL, D_MODEL), 0.5 * D_MODEL ** -0.5),
        'final_norm_w': 1.0 + nrm(34, (D_MODEL,), 0.01),
    }


def reference(x_prompt, x_sample, meta_tokens, norm_w, w_in, mlstm_ig_b, mlstm_fg_b, mlstm_norm_w,
              hgrn_lower_bounds, hgrn_norm_w, rwkv_shift_mu, rwkv_w0, rwkv_w2, rwkv_a0, rwkv_a2,
              rwkv_k_k, rwkv_k_a, rwkv_r_k, rwkv_ln_w, rwkv_ln_b, s5_a_re, s5_a_im, s5_log_dt,
              s5_b_re, s5_b_im, s5_c_re, s5_c_im, s5_d, s5_glu_w, s5_glu_b, w_branch, w_gate, b_gate,
              w_out, final_norm_w):
    lbw = jax.nn.softmax(hgrn_lower_bounds.astype(jnp.float32), axis=1)
    hgrn_lb = jnp.cumsum(lbw, axis=1) - lbw[:, :1]
    p = {
        'meta_tokens': meta_tokens, 'norm_w': norm_w, 'w_in': w_in,
        'mlstm_ig_b': mlstm_ig_b, 'mlstm_fg_b': mlstm_fg_b, 'mlstm_norm_w': mlstm_norm_w,
        'hgrn_lb': hgrn_lb, 'hgrn_norm_w': hgrn_norm_w,
        'rwkv_shift_mu': rwkv_shift_mu, 'rwkv_w0': rwkv_w0, 'rwkv_w2': rwkv_w2, 'rwkv_a0': rwkv_a0,
        'rwkv_a2': rwkv_a2, 'rwkv_k_k': rwkv_k_k, 'rwkv_k_a': rwkv_k_a, 'rwkv_r_k': rwkv_r_k,
        'rwkv_ln_w': rwkv_ln_w, 'rwkv_ln_b': rwkv_ln_b,
        's5_a_re': s5_a_re, 's5_a_im': s5_a_im, 's5_log_dt': s5_log_dt, 's5_b_re': s5_b_re,
        's5_b_im': s5_b_im, 's5_c_re': s5_c_re, 's5_c_im': s5_c_im, 's5_d': s5_d,
        's5_glu_w': s5_glu_w, 's5_glu_b': s5_glu_b,
        'w_branch': w_branch, 'w_gate': w_gate, 'b_gate': b_gate, 'w_out': w_out,
        'final_norm_w': final_norm_w,
    }
    y_prompt = _trunk(x_prompt, p)
    y_sample = _trunk(x_sample, p)
    return (y_prompt, y_sample)
```

```python
import functools

import jax
import jax.numpy as jnp
from jax import lax
from jax.experimental import pallas as pl
from jax.experimental.pallas import tpu as pltpu

F32 = jnp.float32
BF16 = jnp.bfloat16
HI = lax.Precision.HIGHEST

D_MODEL = 2048
DEPTH = 2
N_META = 16
CHUNK = 64
TAIL_PAD = CHUNK - N_META
WIDTH = D_MODEL // 2
RMS_EPS = 1e-6
NEG = -1e30

MLSTM_HEADS = 4
MLSTM_DV = WIDTH // MLSTM_HEADS
MLSTM_DQK = MLSTM_DV // 2
HGRN_EXPAND = 128
HGRN_HEADS = WIDTH // HGRN_EXPAND
RWKV_HEAD = 64
RWKV_HEADS = WIDTH // RWKV_HEAD
RWKV_LORA = 64
RWKV_GN_EPS = 64e-5
S5_GROUP = 16
S5_GROUPS = WIDTH // S5_GROUP
S5_STATE = 64
S5_BLOCK = 16
S5_ROW = S5_BLOCK * S5_GROUP

LANES = 128
VMEM_LIMIT = 48 * 1024 * 1024

C_QA, C_KA, C_VA, C_OA, C_GA = 0, 512, 1024, 2048, 3072
C_QB, C_FB, C_IB, C_GB = 4096, 5120, 7168, 8192
C_RC, C_KC, C_VC, C_GC = 9216, 10240, 11264, 12288
C_UD, C_GD = 13312, 14336
N_MAIN = 15360
N_SMALL = 384
G_OFF = 256


def _params(**kw):
    return pltpu.CompilerParams(vmem_limit_bytes=VMEM_LIMIT, **kw)


def _tile(total, cap, mult):
    best = None
    for t in range(mult, min(total, cap) + 1, mult):
        if total % t == 0:
            best = t
    assert best is not None, (total, cap, mult)
    return best


def _sigmoid(x):
    return jax.nn.sigmoid(x)


def _silu(x):
    return x * jax.nn.sigmoid(x)


def _softplus(x):
    return jnp.maximum(x, 0.0) + jnp.log(1.0 + jnp.exp(-jnp.abs(x)))


def _dot_nt(a, b, precision=None):
    return lax.dot_general(a, b, (((1,), (1,)), ((), ())), precision=precision, preferred_element_type=F32)


def _dot_tn(a, b, precision=None):
    return lax.dot_general(a, b, (((0,), (0,)), ((), ())), precision=precision, preferred_element_type=F32)


def _rmsnorm_body(x_ref, w_ref, o_ref):
    x = x_ref[...]
    y = x * lax.rsqrt(jnp.mean(x * x, axis=-1, keepdims=True) + RMS_EPS) * w_ref[...]
    o_ref[...] = y.astype(o_ref.dtype)


def _rmsnorm(x2d, w, out_dtype):
    m, d = x2d.shape
    tm = _tile(m, 512, 16)
    return pl.pallas_call(
        _rmsnorm_body,
        grid=(m // tm,),
        in_specs=[pl.BlockSpec((tm, d), lambda i: (i, 0)), pl.BlockSpec((1, d), lambda i: (0, 0))],
        out_specs=pl.BlockSpec((tm, d), lambda i: (i, 0)),
        out_shape=jax.ShapeDtypeStruct((m, d), out_dtype),
        compiler_params=_params(dimension_semantics=("parallel",)),
        name="rmsnorm",
    )(x2d, w.reshape(1, d).astype(F32))


def _mm_body(*refs, has_res):
    x_ref, w_ref = refs[0], refs[1]
    o_ref = refs[-1]
    acc = jnp.dot(x_ref[...], w_ref[...], preferred_element_type=F32)
    if has_res:
        acc = refs[2][...] + acc
    o_ref[...] = acc.astype(o_ref.dtype)


def _mm(x, w, res=None, out_dtype=F32, name="mm"):
    m, k = x.shape
    n = w.shape[1]
    tm = _tile(m, 1408, 16)
    tn = 1024 if n % 1024 == 0 else n
    in_specs = [pl.BlockSpec((tm, k), lambda i, j: (i, 0)), pl.BlockSpec((k, tn), lambda i, j: (0, j))]
    args = [x, w]
    if res is not None:
        in_specs.append(pl.BlockSpec((tm, tn), lambda i, j: (i, j)))
        args.append(res)
    return pl.pallas_call(
        functools.partial(_mm_body, has_res=res is not None),
        grid=(m // tm, n // tn),
        in_specs=in_specs,
        out_specs=pl.BlockSpec((tm, tn), lambda i, j: (i, j)),
        out_shape=jax.ShapeDtypeStruct((m, n), out_dtype),
        compiler_params=_params(dimension_semantics=("parallel", "parallel")),
        name=name,
    )(*args)


def _mlstm_body(q_ref, k_ref, v_ref, zs_ref, bias_ref, o_ref, c_sc, n_sc, m_sc, *, nc, lreal):
    d = pl.program_id(1)
    c = pl.program_id(2)
    cc = c + d * (nc - 1 - 2 * c)
    rev = d == 1

    @pl.when(c == 0)
    def _():
        c_sc[...] = jnp.zeros_like(c_sc)
        n_sc[...] = jnp.zeros_like(n_sc)
        m_sc[...] = jnp.zeros_like(m_sc)

    L = CHUNK
    row = lax.broadcasted_iota(jnp.int32, (L, 1), 0)
    valid = (cc * L + row) < lreal
    lane = lax.broadcasted_iota(jnp.int32, (1, LANES), 1)
    g = zs_ref[0][:, G_OFF:G_OFF + LANES] + bias_ref[:, G_OFF:G_OFF + LANES]
    is_f = (lane >= 2 * MLSTM_HEADS) & (lane < 4 * MLSTM_HEADS)
    lf_all = jnp.where(valid & is_f, -_softplus(-g), 0.0)
    r_i = lax.broadcasted_iota(jnp.int32, (L, L), 0)
    c_i = lax.broadcasted_iota(jnp.int32, (L, L), 1)
    sgn = 1 - 2 * d
    causal = (c_i - r_i) * sgn <= 0
    b_all = jnp.dot(causal.astype(F32), lf_all, precision=HI, preferred_element_type=F32)

    q = q_ref[0]
    k = k_ref[0] * (MLSTM_DQK ** -0.5)
    v = v_ref[0]
    for h in range(MLSTM_HEADS):
        fcol = 2 * MLSTM_HEADS + MLSTM_HEADS * d + h
        icol = MLSTM_HEADS * d + h
        b = jnp.sum(jnp.where(lane == fcol, b_all, 0.0), axis=1, keepdims=True)
        li = jnp.sum(jnp.where(lane == icol, g, 0.0), axis=1, keepdims=True)
        li = jnp.where(valid, li, NEG)
        fm = jnp.where(lane == 0, b, jnp.where(lane == 1, 1.0, 0.0))
        gm = jnp.where(lane == 0, 1.0, jnp.where(lane == 1, li - b, 0.0))
        dlog = jnp.where(causal, _dot_nt(fm, gm, HI), NEG)
        m_st = m_sc[h][:, 0:1]
        a = b + m_st
        mt = jnp.maximum(a, jnp.max(dlog, axis=1, keepdims=True))
        qh = q[:, h * MLSTM_DQK:(h + 1) * MLSTM_DQK]
        kh = k[:, h * MLSTM_DQK:(h + 1) * MLSTM_DQK]
        vh = v[:, h * MLSTM_DV:(h + 1) * MLSTM_DV]
        w = jnp.exp(dlog - mt) * _dot_nt(qh, kh, HI)
        inter = jnp.exp(a - mt)
        num = inter * jnp.dot(qh, c_sc[h], precision=HI, preferred_element_type=F32) \
            + jnp.dot(w, vh, precision=HI, preferred_element_type=F32)
        den = inter * jnp.sum(qh * n_sc[h], axis=1, keepdims=True) + jnp.sum(w, axis=1, keepdims=True)
        o_ref[0, 0, :, h * MLSTM_DV:(h + 1) * MLSTM_DV] = num / jnp.maximum(jnp.abs(den), jnp.exp(-mt))
        bl = jnp.where(rev, b[0:1], b[L - 1:L])
        tail = bl - b + li
        m_new = jnp.maximum(bl + m_st, jnp.max(tail, axis=0, keepdims=True))
        kw = kh * jnp.exp(tail - m_new)
        decay = jnp.exp(bl + m_st - m_new)
        c_sc[h] = decay * c_sc[h] + _dot_tn(kw, vh, HI)
        n_sc[h] = decay * n_sc[h] + jnp.sum(kw, axis=0, keepdims=True)
        m_sc[h] = jnp.broadcast_to(m_new, (1, LANES))


def _mlstm_scan(zm, zs, small_bias, lreal):
    n, tp, _ = zm.shape
    nc = tp // CHUNK
    cidx = lambda c, d: c + d * (nc - 1 - 2 * c)
    return pl.pallas_call(
        functools.partial(_mlstm_body, nc=nc, lreal=lreal),
        grid=(n, 2, nc),
        in_specs=[
            pl.BlockSpec((1, CHUNK, 512), lambda i, d, c: (i, cidx(c, d), C_QA // 512)),
            pl.BlockSpec((1, CHUNK, 512), lambda i, d, c: (i, cidx(c, d), C_KA // 512)),
            pl.BlockSpec((1, CHUNK, WIDTH), lambda i, d, c: (i, cidx(c, d), C_VA // WIDTH)),
            pl.BlockSpec((1, CHUNK, N_SMALL), lambda i, d, c: (i, cidx(c, d), 0)),
            pl.BlockSpec((1, N_SMALL), lambda i, d, c: (0, 0)),
        ],
        out_specs=pl.BlockSpec((1, 1, CHUNK, WIDTH), lambda i, d, c: (i, d, cidx(c, d), 0)),
        out_shape=jax.ShapeDtypeStruct((n, 2, tp, WIDTH), F32),
        scratch_shapes=[
            pltpu.VMEM((MLSTM_HEADS, MLSTM_DQK, MLSTM_DV), F32),
            pltpu.VMEM((MLSTM_HEADS, 1, MLSTM_DQK), F32),
            pltpu.VMEM((MLSTM_HEADS, 1, LANES), F32),
        ],
        compiler_params=_params(dimension_semantics=("parallel", "parallel", "arbitrary")),
        name="mlstm_scan",
    )(zm, zm, zm, zs, small_bias)


def _mlstm_post_body(h_ref, oa_ref, ga_ref, nw_ref, o_ref):
    h = h_ref[0, 0] + h_ref[0, 1]
    for j in range(MLSTM_HEADS):
        sl = slice(j * MLSTM_DV, (j + 1) * MLSTM_DV)
        hh = h[:, sl]
        hn = hh * lax.rsqrt(jnp.mean(hh * hh, axis=1, keepdims=True) + RMS_EPS) * nw_ref[:, sl]
        o_ref[0, :, sl] = (hn * _sigmoid(oa_ref[0][:, sl]) * _silu(ga_ref[0][:, sl])).astype(o_ref.dtype)


def _mlstm_post(hd, zm, norm_w):
    n, _, tp, _ = hd.shape
    tr = _tile(tp, 512, 16)
    return pl.pallas_call(
        _mlstm_post_body,
        grid=(n, tp // tr),
        in_specs=[
            pl.BlockSpec((1, 2, tr, WIDTH), lambda i, r: (i, 0, r, 0)),
            pl.BlockSpec((1, tr, WIDTH), lambda i, r: (i, r, C_OA // WIDTH)),
            pl.BlockSpec((1, tr, WIDTH), lambda i, r: (i, r, C_GA // WIDTH)),
            pl.BlockSpec((1, WIDTH), lambda i, r: (0, 0)),
        ],
        out_specs=pl.BlockSpec((1, tr, WIDTH), lambda i, r: (i, r, 0)),
        out_shape=jax.ShapeDtypeStruct((n, tp, WIDTH), BF16),
        compiler_params=_params(dimension_semantics=("parallel", "parallel")),
        name="mlstm_post",
    )(hd, zm, zm, norm_w.reshape(1, WIDTH))


def _hgrn_body(q_ref, f_ref, v_ref, lb_ref, o_ref, st_sc, b_sc, k_sc, *, nc, lreal):
    d = pl.program_id(1)
    c = pl.program_id(3)
    cc = c + d * (nc - 1 - 2 * c)
    rev = d == 1

    @pl.when(c == 0)
    def _():
        st_sc[...] = jnp.zeros_like(st_sc)

    L = CHUNK
    row = lax.broadcasted_iota(jnp.int32, (L, 1), 0)
    valid = (cc * L + row) < lreal
    q = _silu(q_ref[0])
    fpre = f_ref[0]
    lb = lb_ref[0]
    log_f = jnp.where(valid, jnp.log(lb + (1.0 - lb) * _sigmoid(fpre)), 0.0)
    kk = jnp.where(valid, (1.0 - lb) * _sigmoid(-fpre), 0.0)
    r_i = lax.broadcasted_iota(jnp.int32, (L, L), 0)
    c_i = lax.broadcasted_iota(jnp.int32, (L, L), 1)
    sgn = 1 - 2 * d
    causal = (c_i - r_i) * sgn <= 0
    b = jnp.dot(causal.astype(F32), log_f, precision=HI, preferred_element_type=F32)
    b_sc[...] = b
    k_sc[...] = kk

    def pair(s, acc):
        bs = b_sc[pl.ds(s, 1), :]
        ks = k_sc[pl.ds(s, 1), :]
        vs = v_ref[0, pl.ds(s, 1), :]
        seen = (row - s) * sgn >= 0
        e = jnp.exp(jnp.where(seen, b - bs, NEG))
        return acc + jnp.sum(q * e * ks, axis=1, keepdims=True) * vs

    o = lax.fori_loop(0, L, pair, jnp.zeros((L, HGRN_EXPAND), F32))
    bl = jnp.where(rev, b[0:1], b[L - 1:L])
    st = st_sc[...]
    o_ref[0, 0] = o + _dot_nt(q * jnp.exp(b), st, HI)
    st_sc[...] = st * jnp.exp(bl) + _dot_tn(v_ref[0], kk * jnp.exp(bl - b), HI)


def _hgrn_scan(zm, lb, lreal):
    n, tp, _ = zm.shape
    nc = tp // CHUNK
    cidx = lambda c, d: c + d * (nc - 1 - 2 * c)
    w = HGRN_EXPAND
    return pl.pallas_call(
        functools.partial(_hgrn_body, nc=nc, lreal=lreal),
        grid=(n, 2, HGRN_HEADS, nc),
        in_specs=[
            pl.BlockSpec((1, CHUNK, w), lambda i, d, h, c: (i, cidx(c, d), C_QB // w + h)),
            pl.BlockSpec((1, CHUNK, w), lambda i, d, h, c: (i, cidx(c, d), C_FB // w + HGRN_HEADS * d + h)),
            pl.BlockSpec((1, CHUNK, w), lambda i, d, h, c: (i, cidx(c, d), C_IB // w + h)),
            pl.BlockSpec((1, 1, w), lambda i, d, h, c: (d, 0, h)),
        ],
        out_specs=pl.BlockSpec((1, 1, CHUNK, w), lambda i, d, h, c: (i, d, cidx(c, d), h)),
        out_shape=jax.ShapeDtypeStruct((n, 2, tp, WIDTH), F32),
        scratch_shapes=[pltpu.VMEM((w, w), F32), pltpu.VMEM((CHUNK, w), F32), pltpu.VMEM((CHUNK, w), F32)],
        compiler_params=_params(dimension_semantics=("parallel", "parallel", "parallel", "arbitrary")),
        name="hgrn_scan",
    )(zm, zm, zm, lb)


def _hgrn_post_body(o_ref, gb_ref, nw_ref, y_ref):
    o = o_ref[0, 0] + o_ref[0, 1]
    for j in range(HGRN_HEADS):
        sl = slice(j * HGRN_EXPAND, (j + 1) * HGRN_EXPAND)
        oh = o[:, sl]
        on = oh * lax.rsqrt(jnp.mean(oh * oh, axis=1, keepdims=True) + RMS_EPS) * nw_ref[...]
        y_ref[0, :, sl] = (on * _silu(gb_ref[0][:, sl])).astype(y_ref.dtype)


def _hgrn_post(od, zm, norm_w):
    n, _, tp, _ = od.shape
    tr = _tile(tp, 512, 16)
    return pl.pallas_call(
        _hgrn_post_body,
        grid=(n, tp // tr),
        in_specs=[
            pl.BlockSpec((1, 2, tr, WIDTH), lambda i, r: (i, 0, r, 0)),
            pl.BlockSpec((1, tr, WIDTH), lambda i, r: (i, r, C_GB // WIDTH)),
            pl.BlockSpec((1, HGRN_EXPAND), lambda i, r: (0, 0)),
        ],
        out_specs=pl.BlockSpec((1, tr, WIDTH), lambda i, r: (i, r, 0)),
        out_shape=jax.ShapeDtypeStruct((n, tp, WIDTH), BF16),
        compiler_params=_params(dimension_semantics=("parallel", "parallel")),
        name="hgrn_post",
    )(od, zm, norm_w.reshape(1, HGRN_EXPAND))


def _shift_body(x_ref, mu_ref, o_ref, *, lreal):
    x = x_ref[0]
    tp = x.shape[0]
    row = lax.broadcasted_iota(jnp.int32, (tp, 1), 0)
    prev = jnp.where(row == 0, 0.0, pltpu.roll(x, 1, 0))
    nxt = jnp.where(row + 1 >= lreal, 0.0, pltpu.roll(x, tp - 1, 0))
    o_ref[0] = x + mu_ref[0:1, :] * (prev - x) + mu_ref[1:2, :] * (nxt - x)


def _token_shift(z, col0, ncols, mu, lreal):
    n, tp, _ = z.shape
    tc = LANES
    return pl.pallas_call(
        functools.partial(_shift_body, lreal=lreal),
        grid=(n, ncols // tc),
        in_specs=[
            pl.BlockSpec((1, tp, tc), lambda i, j: (i, 0, col0 // tc + j)),
            pl.BlockSpec((2, tc), lambda i, j: (0, j)),
        ],
        out_specs=pl.BlockSpec((1, tp, tc), lambda i, j: (i, 0, j)),
        out_shape=jax.ShapeDtypeStruct((n, tp, ncols), F32),
        compiler_params=_params(dimension_semantics=("parallel", "parallel")),
        name="token_shift",
    )(z, mu)


def _seg_ones():
    r = lax.broadcasted_iota(jnp.int32, (LANES, LANES), 0) // RWKV_HEAD
    c = lax.broadcasted_iota(jnp.int32, (LANES, LANES), 1) // RWKV_HEAD
    return (r == c).astype(F32)


def _head_sum(x, seg):
    parts = [jnp.dot(x[:, j * LANES:(j + 1) * LANES], seg, precision=HI, preferred_element_type=F32)
             for j in range(WIDTH // LANES)]
    return jnp.concatenate(parts, axis=1)


def _rwkv_pre_body(f_ref, fs_ref, w2_ref, a2_ref, w0_ref, a0_ref, kk_ref, ka_ref, rk_ref,
                   r_o, v_o, kk_o, wf_o, wb_o, nf_o, nb_o, kf_o, kb_o, bonus_o):
    f = f_ref[0]
    r = f[:, 0:WIDTH]
    k = f[:, WIDTH:2 * WIDTH]
    v = f[:, 2 * WIDTH:3 * WIDTH]
    fs = fs_ref[0]
    wl = fs[:, 0:LANES]
    al = fs[:, LANES:2 * LANES]
    seg = _seg_ones()
    w = w0_ref[...] + jnp.dot(jnp.tanh(wl), w2_ref[...], precision=HI, preferred_element_type=F32)
    wdec = jnp.exp(-jnp.exp(-_softplus(-w) - 0.5))
    a = _sigmoid(a0_ref[...] + jnp.dot(al, a2_ref[...], precision=HI, preferred_element_type=F32))
    kk = k * kk_ref[...]
    kk = kk / jnp.maximum(jnp.sqrt(_head_sum(kk * kk, seg)), 1e-12)
    a_f, a_b = a[:, 0:WIDTH], a[:, WIDTH:2 * WIDTH]
    kd_f = k * (1.0 + (a_f - 1.0) * ka_ref[...])
    kd_b = k * (1.0 + (a_b - 1.0) * ka_ref[...])
    r_o[0] = r
    v_o[0] = v
    kk_o[0] = kk
    wf_o[0] = wdec[:, 0:WIDTH]
    wb_o[0] = wdec[:, WIDTH:2 * WIDTH]
    nf_o[0] = -(kk * a_f)
    nb_o[0] = -(kk * a_b)
    kf_o[0] = kd_f
    kb_o[0] = kd_b
    bonus_o[0] = _head_sum(r * (kd_f + kd_b) * rk_ref[...], seg) * v


def _rwkv_pre(fsh, fsm, w2cat, a2cat, w0, a0, k_k, k_a, r_k):
    n, tp, _ = fsh.shape
    tr = _tile(tp, 256, 8)
    row = lambda width: pl.BlockSpec((1, width), lambda i, r: (0, 0))
    out_spec = pl.BlockSpec((1, tr, WIDTH), lambda i, r: (i, r, 0))
    return pl.pallas_call(
        _rwkv_pre_body,
        grid=(n, tp // tr),
        in_specs=[
            pl.BlockSpec((1, tr, 3 * WIDTH), lambda i, r: (i, r, 0)),
            pl.BlockSpec((1, tr, 2 * LANES), lambda i, r: (i, r, 0)),
            pl.BlockSpec((LANES, 2 * WIDTH), lambda i, r: (0, 0)),
            pl.BlockSpec((LANES, 2 * WIDTH), lambda i, r: (0, 0)),
            row(2 * WIDTH), row(2 * WIDTH), row(WIDTH), row(WIDTH), row(WIDTH),
        ],
        out_specs=[out_spec] * 10,
        out_shape=[jax.ShapeDtypeStruct((n, tp, WIDTH), F32)] * 10,
        compiler_params=_params(dimension_semantics=("parallel", "parallel")),
        name="rwkv_pre",
    )(fsh, fsm, w2cat, a2cat, w0, a0, k_k, k_a, r_k)


def _rwkv_scan_body(w_ref, kk_ref, n_ref, kd_ref, r_ref, v_ref, y_ref, s_sc, *, tb, nv):
    @pl.when(pl.program_id(1) == 0)
    def _():
        s_sc[...] = jnp.zeros_like(s_sc)

    hk = RWKV_HEAD

    def step(t, carry):
        base = t * hk

        def p1(kq, acc):
            out = acc
            for u in range(4):
                kx = kq * 4 + u
                out = out + s_sc[kx] * kk_ref[0, pl.ds(base + kx, 1), :]
            return out

        sa = lax.fori_loop(0, hk // 4, p1, jnp.zeros((nv, LANES), F32))
        vt = v_ref[0, pl.ds(t * nv, nv), :]

        def p2(kq, acc):
            out = acc
            for u in range(4):
                kx = kq * 4 + u
                s = (s_sc[kx] * w_ref[0, pl.ds(base + kx, 1), :]
                     + sa * n_ref[0, pl.ds(base + kx, 1), :]
                     + vt * kd_ref[0, pl.ds(base + kx, 1), :])
                s_sc[kx] = s
                out = out + s * r_ref[0, pl.ds(base + kx, 1), :]
            return out

        y_ref[0, pl.ds(t * nv, nv), :] = lax.fori_loop(0, hk // 4, p2, jnp.zeros((nv, LANES), F32))
        return carry

    lax.fori_loop(0, tb, step, 0)


def _rwkv_scan(ops, v, nv):
    g, rows, _ = ops[0].shape
    lr = rows // RWKV_HEAD
    tb = _tile(lr, 16, 1)
    kspec = pl.BlockSpec((1, tb * RWKV_HEAD, LANES), lambda i, t: (i, t, 0))
    vspec = pl.BlockSpec((1, tb * nv, LANES), lambda i, t: (i, t, 0))
    return pl.pallas_call(
        functools.partial(_rwkv_scan_body, tb=tb, nv=nv),
        grid=(g, lr // tb),
        in_specs=[kspec] * 5 + [vspec],
        out_specs=vspec,
        out_shape=jax.ShapeDtypeStruct((g, lr * nv, LANES), F32),
        scratch_shapes=[pltpu.VMEM((RWKV_HEAD, nv, LANES), F32)],
        compiler_params=_params(dimension_semantics=("parallel", "arbitrary")),
        name="rwkv_scan",
    )(*ops, v)


def _rwkv_post_body(yf_ref, yb_ref, bonus_ref, gc_ref, lnw_ref, lnb_ref, o_ref):
    seg = _seg_ones()
    y = yf_ref[0] + yb_ref[0]
    mean = _head_sum(y, seg) * (1.0 / RWKV_HEAD)
    yc = y - mean
    var = _head_sum(yc * yc, seg) * (1.0 / RWKV_HEAD)
    yn = yc * lax.rsqrt(var + RWKV_GN_EPS) * lnw_ref[...] + lnb_ref[...]
    o_ref[0] = ((yn + bonus_ref[0]) * _silu(gc_ref[0])).astype(o_ref.dtype)


def _rwkv_post(yf, yb, bonus, zm, ln_w, ln_b):
    n, tp, _ = yf.shape
    tr = _tile(tp, 512, 16)
    spec = pl.BlockSpec((1, tr, WIDTH), lambda i, r: (i, r, 0))
    row = pl.BlockSpec((1, WIDTH), lambda i, r: (0, 0))
    return pl.pallas_call(
        _rwkv_post_body,
        grid=(n, tp // tr),
        in_specs=[spec, spec, spec, pl.BlockSpec((1, tr, WIDTH), lambda i, r: (i, r, C_GC // WIDTH)), row, row],
        out_specs=spec,
        out_shape=jax.ShapeDtypeStruct((n, tp, WIDTH), BF16),
        compiler_params=_params(dimension_semantics=("parallel", "parallel")),
        name="rwkv_post",
    )(yf, yb, bonus, zm, ln_w.reshape(1, WIDTH), ln_b.reshape(1, WIDTH))


def _to_chains_prompt(x, lreal, flip):
    n = x.shape[0]
    x = x[:, :lreal]
    if flip:
        x = jnp.flip(x, axis=1)
    x = x.reshape(n, lreal, RWKV_HEADS, RWKV_HEAD)
    return jnp.transpose(x, (1, 3, 0, 2)).reshape(lreal * RWKV_HEAD, n * RWKV_HEADS)


def _from_chains_prompt(y, n, lreal, tp, flip):
    y = y.reshape(lreal, RWKV_HEAD, n, RWKV_HEADS)
    y = jnp.transpose(y, (2, 0, 3, 1)).reshape(n, lreal, WIDTH)
    if flip:
        y = jnp.flip(y, axis=1)
    return jnp.pad(y, ((0, 0), (0, tp - lreal), (0, 0)))


RWKV_VSPLIT = LANES // (2 * RWKV_HEADS)
RWKV_NV1 = RWKV_HEAD // RWKV_VSPLIT


def _to_chains_single(xf, xb, lreal, is_value):
    xs = jnp.stack([xf[0, :lreal], jnp.flip(xb[0, :lreal], axis=0)])
    if is_value:
        x = xs.reshape(2, lreal, RWKV_HEADS, RWKV_VSPLIT, RWKV_NV1)
        x = jnp.transpose(x, (1, 4, 0, 2, 3))
        return x.reshape(lreal * RWKV_NV1, LANES)
    x = xs.reshape(2, lreal, RWKV_HEADS, RWKV_HEAD)
    x = jnp.transpose(x, (1, 3, 0, 2))
    x = jnp.broadcast_to(x[..., None], x.shape + (RWKV_VSPLIT,))
    return x.reshape(lreal * RWKV_HEAD, LANES)


def _from_chains_single(y, lreal, tp):
    y = y.reshape(lreal, RWKV_NV1, 2, RWKV_HEADS, RWKV_VSPLIT)
    y = jnp.transpose(y, (2, 0, 3, 4, 1)).reshape(2, lreal, WIDTH)
    yf = y[0]
    yb = jnp.flip(y[1], axis=0)
    pad = ((0, tp - lreal), (0, 0))
    return jnp.pad(yf, pad)[None], jnp.pad(yb, pad)[None]


def _rwkv_branch(zm, zs, p, lreal):
    n, tp, _ = zm.shape
    fsh = _token_shift(zm, C_RC, 3 * WIDTH, p['mu_main'], lreal)
    fsm = _token_shift(zs, 0, 2 * LANES, p['mu_small'], lreal)
    r, v, kk, wf, wb, nf, nb, kf, kb, bonus = _rwkv_pre(
        fsh, fsm, p['w2cat'], p['a2cat'], p['w0'], p['a0'], p['k_k'], p['k_a'], p['r_k'])
    if n * RWKV_HEADS == LANES:
        ops = []
        for xf_, xb_ in ((wf, wb), (kk, kk), (nf, nb), (kf, kb), (r, r)):
            ops.append(jnp.stack([_to_chains_prompt(xf_, lreal, False), _to_chains_prompt(xb_, lreal, True)]))
        vv = jnp.stack([_to_chains_prompt(v, lreal, False), _to_chains_prompt(v, lreal, True)])
        y = _rwkv_scan(ops, vv, RWKV_HEAD)
        yf = _from_chains_prompt(y[0], n, lreal, tp, False)
        yb = _from_chains_prompt(y[1], n, lreal, tp, True)
    else:
        assert n == 1
        ops = [_to_chains_single(xf_, xb_, lreal, False)[None]
               for xf_, xb_ in ((wf, wb), (kk, kk), (nf, nb), (kf, kb), (r, r))]
        vv = _to_chains_single(v, v, lreal, True)[None]
        y = _rwkv_scan(ops, vv, RWKV_NV1)
        yf, yb = _from_chains_single(y[0], lreal, tp)
    return _rwkv_post(yf, yb, bonus, zm, p['ln_w'], p['ln_b'])


def _s5_core_body(u_ref, m_ref, n_ref, q_ref, pw_ref, cidx_ref, y_ref, *, nsteps):
    u = u_ref[0, 0]
    y = jnp.dot(u, m_ref[0, 0], precision=HI, preferred_element_type=F32)
    x = jnp.dot(u, n_ref[0, 0], precision=HI, preferred_element_type=F32)
    cidx = cidx_ref[...]
    rows = x.shape[0]
    for j in range(nsteps):
        s = 1 << j
        xs = jnp.where(cidx >= s, pltpu.roll(x, s, 0), 0.0)
        x = x + xs * pw_ref[0, 0, 2 * j:2 * j + 1, :] + pltpu.roll(xs, S5_STATE, 1) * pw_ref[0, 0, 2 * j + 1:2 * j + 2, :]
    xin = jnp.where(cidx >= 1, pltpu.roll(x, 1, 0), 0.0)
    del rows
    y_ref[0, 0] = y + jnp.dot(xin, q_ref[0, 0], precision=HI, preferred_element_type=F32)


def _s5_core(u, mats, cidx, nsteps):
    _, g, rows, _ = u.shape
    mm, nn, qq, pw = mats
    return pl.pallas_call(
        functools.partial(_s5_core_body, nsteps=nsteps),
        grid=(2, g),
        in_specs=[
            pl.BlockSpec((1, 1, rows, S5_ROW), lambda d, j: (d, j, 0, 0)),
            pl.BlockSpec((1, 1, S5_ROW, S5_ROW), lambda d, j: (d, j, 0, 0)),
            pl.BlockSpec((1, 1, S5_ROW, 2 * S5_STATE), lambda d, j: (d, j, 0, 0)),
            pl.BlockSpec((1, 1, 2 * S5_STATE, S5_ROW), lambda d, j: (d, j, 0, 0)),
            pl.BlockSpec((1, 1, pw.shape[2], 2 * S5_STATE), lambda d, j: (d, j, 0, 0)),
            pl.BlockSpec((rows, 1), lambda d, j: (0, 0)),
        ],
        out_specs=pl.BlockSpec((1, 1, rows, S5_ROW), lambda d, j: (d, j, 0, 0)),
        out_shape=jax.ShapeDtypeStruct((2, g, rows, S5_ROW), F32),
        compiler_params=_params(dimension_semantics=("parallel", "parallel")),
        name="s5_core",
    )(u, mm, nn, qq, pw, cidx)


def _s5_post_body(y_ref, ud_ref, gd_ref, d_ref, w_ref, b_ref, o_ref):
    y = y_ref[0] + d_ref[...] * ud_ref[0]
    g = 0.5 * y * (1.0 + jnp.tanh(0.7978845608028654 * (y + 0.044715 * (y * y * y))))
    glu = jnp.dot(g.astype(BF16), w_ref[...], preferred_element_type=F32) + b_ref[...]
    o_ref[0] = (g * _sigmoid(glu) * _silu(gd_ref[0])).astype(o_ref.dtype)


def _s5_post(ys, zm, d, glu_w, glu_b):
    n, tp, _ = ys.shape
    tr = _tile(tp, 512, 16)
    spec = pl.BlockSpec((1, tr, WIDTH), lambda i, r: (i, r, 0))
    row = pl.BlockSpec((1, WIDTH), lambda i, r: (0, 0))
    return pl.pallas_call(
        _s5_post_body,
        grid=(n, tp // tr),
        in_specs=[spec,
                  pl.BlockSpec((1, tr, WIDTH), lambda i, r: (i, r, C_UD // WIDTH)),
                  pl.BlockSpec((1, tr, WIDTH), lambda i, r: (i, r, C_GD // WIDTH)),
                  row, pl.BlockSpec((WIDTH, WIDTH), lambda i, r: (0, 0)), row],
        out_specs=spec,
        out_shape=jax.ShapeDtypeStruct((n, tp, WIDTH), BF16),
        compiler_params=_params(dimension_semantics=("parallel", "parallel")),
        name="s5_post",
    )(ys, zm, zm, d.reshape(1, WIDTH), glu_w, glu_b.reshape(1, WIDTH))


def _s5_mats(a_re, a_im, log_dt, b_re, b_im, c_re, c_im, max_blocks):
    dt = jnp.exp(log_dt)[..., None]
    mag = jnp.exp(a_re * dt)
    ang = a_im * dt
    ab_re, ab_im = mag * jnp.cos(ang), mag * jnp.sin(ang)
    den = a_re * a_re + a_im * a_im
    xr, yi = ab_re - 1.0, ab_im
    coef_re = (xr * a_re + yi * a_im) / den
    coef_im = (yi * a_re - xr * a_im) / den
    bb_re = coef_re[..., None] * b_re - coef_im[..., None] * b_im
    bb_im = coef_re[..., None] * b_im + coef_im[..., None] * b_re
    pr, pi = [jnp.ones_like(ab_re)], [jnp.zeros_like(ab_re)]
    for _ in range(S5_BLOCK):
        pr, pi = pr + [pr[-1] * ab_re - pi[-1] * ab_im], pi + [pr[-1] * ab_im + pi[-1] * ab_re]
    pw_re, pw_im = jnp.stack(pr), jnp.stack(pi)
    t_re = pw_re[..., None] * bb_re - pw_im[..., None] * bb_im
    t_im = pw_re[..., None] * bb_im + pw_im[..., None] * bb_re
    kj = (jnp.einsum('dgcp,jdgpe->jdgce', c_re, t_re, precision=HI)
          - jnp.einsum('dgcp,jdgpe->jdgce', c_im, t_im, precision=HI))
    s_i = jnp.arange(S5_BLOCK)[:, None]
    i_i = jnp.arange(S5_BLOCK)[None, :]
    lag = jnp.clip(i_i - s_i, 0, S5_BLOCK)
    m6 = jnp.where((i_i >= s_i)[:, :, None, None, None, None], kj[lag], 0.0)
    mm = jnp.transpose(m6, (2, 3, 0, 5, 1, 4)).reshape(2, S5_GROUPS, S5_ROW, S5_ROW)
    rev = jnp.arange(S5_BLOCK - 1, -1, -1)
    n_re = jnp.transpose(t_re[rev], (1, 2, 0, 4, 3)).reshape(2, S5_GROUPS, S5_ROW, S5_STATE)
    n_im = jnp.transpose(t_im[rev], (1, 2, 0, 4, 3)).reshape(2, S5_GROUPS, S5_ROW, S5_STATE)
    nn = jnp.concatenate([n_re, n_im], axis=-1)
    ca_re = c_re[None] * pw_re[1:, :, :, None, :] - c_im[None] * pw_im[1:, :, :, None, :]
    ca_im = c_re[None] * pw_im[1:, :, :, None, :] + c_im[None] * pw_re[1:, :, :, None, :]
    q_re = jnp.transpose(ca_re, (1, 2, 4, 0, 3)).reshape(2, S5_GROUPS, S5_STATE, S5_ROW)
    q_im = jnp.transpose(-ca_im, (1, 2, 4, 0, 3)).reshape(2, S5_GROUPS, S5_STATE, S5_ROW)
    qq = jnp.concatenate([q_re, q_im], axis=2)
    sr, si = pw_re[S5_BLOCK], pw_im[S5_BLOCK]
    rows = []
    steps = 0
    while (1 << steps) < max_blocks:
        rows += [jnp.concatenate([sr, sr], axis=-1), jnp.concatenate([-si, si], axis=-1)]
        sr, si = sr * sr - si * si, 2.0 * sr * si
        steps += 1
    while len(rows) % 8:
        rows.append(jnp.zeros_like(rows[0]))
    pw = jnp.stack(rows, axis=2)
    return (mm, nn, qq, pw), steps


def _s5_branch(zm, p, lreal):
    n, tp, _ = zm.shape
    nb = lreal // S5_BLOCK
    ud = zm[:, :lreal, C_UD:C_UD + WIDTH]

    def blocks(x):
        x = x.reshape(n, nb, S5_BLOCK, S5_GROUPS, S5_GROUP)
        return jnp.transpose(x, (3, 0, 1, 2, 4)).reshape(S5_GROUPS, n * nb, S5_ROW)

    rows = n * nb
    rows_p = -(-rows // 8) * 8
    u = jnp.stack([blocks(ud), blocks(jnp.flip(ud, axis=1))])
    u = jnp.pad(u, ((0, 0), (0, 0), (0, rows_p - rows), (0, 0)))
    cidx = jnp.pad(jnp.tile(jnp.arange(nb, dtype=jnp.int32), n), (0, rows_p - rows)).reshape(rows_p, 1)
    mats, nsteps = p['s5_mats'](nb)
    y = _s5_core(u, mats, cidx, nsteps)[:, :, :rows]

    def unblocks(x):
        x = x.reshape(S5_GROUPS, n, nb, S5_BLOCK, S5_GROUP)
        return jnp.transpose(x, (1, 2, 3, 0, 4)).reshape(n, lreal, WIDTH)

    ys = unblocks(y[0]) + jnp.flip(unblocks(y[1]), axis=1)
    ys = jnp.pad(ys, ((0, 0), (0, tp - lreal), (0, 0)))
    return _s5_post(ys, zm, p['s5_d'], p['glu_w'], p['glu_b'])


def _merge_body(h_ref, ya_ref, yb_ref, yc_ref, yd_ref, wg_ref, bg_ref, wb_ref, o_ref):
    h = h_ref[...]
    acc = None
    for b, y_ref in enumerate((ya_ref, yb_ref, yc_ref, yd_ref)):
        gate = _sigmoid(jnp.dot(h, wg_ref[b], preferred_element_type=F32) + bg_ref[b])
        term = gate * jnp.dot(y_ref[...], wb_ref[b], preferred_element_type=F32)
        acc = term if acc is None else acc + term
    o_ref[...] = acc.astype(o_ref.dtype)


def _merge(h, ys, wg, bg, wb):
    m, d = h.shape
    tm = _tile(m, 1024, 16)
    tn = 256
    yspec = pl.BlockSpec((tm, WIDTH), lambda i, j: (i, 0))
    return pl.pallas_call(
        _merge_body,
        grid=(m // tm, d // tn),
        in_specs=[pl.BlockSpec((tm, d), lambda i, j: (i, 0)), yspec, yspec, yspec, yspec,
                  pl.BlockSpec((4, d, tn), lambda i, j: (0, 0, j)),
                  pl.BlockSpec((4, 1, tn), lambda i, j: (0, 0, j)),
                  pl.BlockSpec((4, WIDTH, tn), lambda i, j: (0, 0, j))],
        out_specs=pl.BlockSpec((tm, tn), lambda i, j: (i, j)),
        out_shape=jax.ShapeDtypeStruct((m, d), BF16),
        compiler_params=_params(dimension_semantics=("parallel", "parallel")),
        name="merge",
    )(h, *ys, wg, bg, wb)


def _layer(xp, p, lreal):
    n, tp, d = xp.shape
    m = n * tp
    x2 = xp.reshape(m, d)
    h = _rmsnorm(x2, p['norm_w'], BF16)
    zm = _mm(h, p['w_main'], name="proj_main").reshape(n, tp, N_MAIN)
    zs = _mm(h, p['w_small'], name="proj_small").reshape(n, tp, N_SMALL)
    ya = _mlstm_post(_mlstm_scan(zm, zs, p['small_bias'], lreal), zm, p['mlstm_norm_w'])
    yb = _hgrn_post(_hgrn_scan(zm, p['hgrn_lb'], lreal), zm, p['hgrn_norm_w'])
    yc = _rwkv_branch(zm, zs, p, lreal)
    yd = _s5_branch(zm, p, lreal)
    ys = [y.reshape(m, WIDTH) for y in (ya, yb, yc, yd)]
    merged = _merge(h, ys, p['w_gate'], p['b_gate'], p['w_branch'])
    return _mm(merged, p['w_out'], res=x2, name="proj_out").reshape(n, tp, d)


def _trunk(x, meta, layers, final_norm_w):
    n, t, d = x.shape
    lreal = t + N_META
    tp = lreal + TAIL_PAD
    xp = jnp.concatenate([jnp.broadcast_to(meta[None], (n, N_META, d)), x, jnp.zeros((n, TAIL_PAD, d), x.dtype)], axis=1)
    for p in layers:
        xp = _layer(xp, p, lreal)
    y = _rmsnorm(xp.reshape(n * tp, d), final_norm_w, F32).reshape(n, tp, d)
    return y[:, N_META:lreal]


def _layer_params(l, a):
    w_in = a['w_in'][l]
    offs = {}
    acc = 0
    sizes = (512, 512, WIDTH, WIDTH, 8, 8, WIDTH, WIDTH, 2 * WIDTH, WIDTH, WIDTH, WIDTH, WIDTH, WIDTH,
             2 * RWKV_LORA, 2 * RWKV_LORA, WIDTH, WIDTH, WIDTH)
    names = ('qa', 'ka', 'va', 'oa', 'iga', 'fga', 'ga', 'qb', 'fb', 'ib', 'gb', 'rc', 'kc', 'vc', 'wlc', 'alc',
             'gc', 'ud', 'gd')
    for nm, sz in zip(names, sizes):
        offs[nm] = (acc, acc + sz)
        acc += sz
    col = lambda nm: w_in[:, offs[nm][0]:offs[nm][1]]
    main_order = ('qa', 'ka', 'va', 'oa', 'ga', 'qb', 'fb', 'ib', 'gb', 'rc', 'kc', 'vc', 'gc', 'ud', 'gd')
    w_main = jnp.concatenate([col(nm) for nm in main_order], axis=1).astype(BF16)
    w_small = jnp.concatenate([col('wlc'), col('alc'), col('iga'), col('fga'),
                               jnp.zeros((D_MODEL, LANES - 16), F32)], axis=1).astype(BF16)
    small_bias = jnp.concatenate([jnp.zeros((G_OFF,), F32), a['mlstm_ig_b'][l].reshape(-1),
                                  a['mlstm_fg_b'][l].reshape(-1), jnp.zeros((LANES - 16,), F32)]).reshape(1, N_SMALL)
    lbw = jax.nn.softmax(a['hgrn_lower_bounds'].astype(F32), axis=1)
    hgrn_lb = (jnp.cumsum(lbw, axis=1) - lbw[:, :1])[:, l].reshape(2, 1, WIDTH)
    mu = a['rwkv_shift_mu'][l]
    zero = jnp.zeros((RWKV_LORA, WIDTH), F32)
    blockdiag = lambda w2: jnp.concatenate([jnp.concatenate([w2[0], zero], axis=1),
                                            jnp.concatenate([zero, w2[1]], axis=1)], axis=0)
    s5_args = tuple(a[k][l].astype(F32) for k in ('s5_a_re', 's5_a_im', 's5_log_dt', 's5_b_re', 's5_b_im',
                                                  's5_c_re', 's5_c_im'))
    return {
        'norm_w': a['norm_w'][l], 'w_main': w_main, 'w_small': w_small, 'small_bias': small_bias,
        'mlstm_norm_w': a['mlstm_norm_w'][l], 'hgrn_lb': hgrn_lb, 'hgrn_norm_w': a['hgrn_norm_w'][l],
        'mu_main': mu[:, :3 * WIDTH], 'mu_small': mu[:, 3 * WIDTH:],
        'w2cat': blockdiag(a['rwkv_w2'][l]), 'a2cat': blockdiag(a['rwkv_a2'][l]),
        'w0': a['rwkv_w0'][l].reshape(1, 2 * WIDTH), 'a0': a['rwkv_a0'][l].reshape(1, 2 * WIDTH),
        'k_k': a['rwkv_k_k'][l].reshape(1, WIDTH), 'k_a': a['rwkv_k_a'][l].reshape(1, WIDTH),
        'r_k': a['rwkv_r_k'][l].reshape(1, WIDTH),
        'ln_w': a['rwkv_ln_w'][l], 'ln_b': a['rwkv_ln_b'][l],
        's5_mats': functools.partial(_s5_mats, *s5_args),
        's5_d': a['s5_d'][l], 'glu_w': a['s5_glu_w'][l].astype(BF16), 'glu_b': a['s5_glu_b'][l],
        'w_gate': a['w_gate'][l].astype(BF16), 'b_gate': a['b_gate'][l].reshape(4, 1, D_MODEL),
        'w_branch': a['w_branch'][l].astype(BF16), 'w_out': a['w_out'][l].astype(BF16),
    }


def kernel(x_prompt, x_sample, meta_tokens, norm_w, w_in, mlstm_ig_b, mlstm_fg_b, mlstm_norm_w, hgrn_lower_bounds, hgrn_norm_w, rwkv_shift_mu, rwkv_w0, rwkv_w2, rwkv_a0, rwkv_a2, rwkv_k_k, rwkv_k_a, rwkv_r_k, rwkv_ln_w, rwkv_ln_b, s5_a_re, s5_a_im, s5_log_dt, s5_b_re, s5_b_im, s5_c_re, s5_c_im, s5_d, s5_glu_w, s5_glu_b, w_branch, w_gate, b_gate, w_out, final_norm_w):
    a = dict(norm_w=norm_w, w_in=w_in, mlstm_ig_b=mlstm_ig_b, mlstm_fg_b=mlstm_fg_b, mlstm_norm_w=mlstm_norm_w,
             hgrn_lower_bounds=hgrn_lower_bounds, hgrn_norm_w=hgrn_norm_w, rwkv_shift_mu=rwkv_shift_mu,
             rwkv_w0=rwkv_w0, rwkv_w2=rwkv_w2, rwkv_a0=rwkv_a0, rwkv_a2=rwkv_a2, rwkv_k_k=rwkv_k_k,
             rwkv_k_a=rwkv_k_a, rwkv_r_k=rwkv_r_k, rwkv_ln_w=rwkv_ln_w, rwkv_ln_b=rwkv_ln_b, s5_a_re=s5_a_re,
             s5_a_im=s5_a_im, s5_log_dt=s5_log_dt, s5_b_re=s5_b_re, s5_b_im=s5_b_im, s5_c_re=s5_c_re,
             s5_c_im=s5_c_im, s5_d=s5_d, s5_glu_w=s5_glu_w, s5_glu_b=s5_glu_b, w_branch=w_branch, w_gate=w_gate,
             b_gate=b_gate, w_out=w_out)
    layers = [_layer_params(l, a) for l in range(DEPTH)]
    meta = meta_tokens.astype(x_prompt.dtype)
    return (_trunk(x_prompt, meta, layers, final_norm_w), _trunk(x_sample, meta, layers, final_norm_w))
```

```python
import functools

import jax
import jax.numpy as jnp
from jax import lax
from jax.experimental import pallas as pl
from jax.experimental.pallas import tpu as pltpu

F32 = jnp.float32
BF16 = jnp.bfloat16
HI = lax.Precision.HIGHEST

D_MODEL = 2048
DEPTH = 2
N_META = 16
CHUNK = 64
TAIL_PAD = CHUNK - N_META
WIDTH = D_MODEL // 2
RMS_EPS = 1e-6
NEG = -1e30

MLSTM_HEADS = 4
MLSTM_DV = WIDTH // MLSTM_HEADS
MLSTM_DQK = MLSTM_DV // 2
HGRN_EXPAND = 128
HGRN_HEADS = WIDTH // HGRN_EXPAND
RWKV_HEAD = 64
RWKV_HEADS = WIDTH // RWKV_HEAD
RWKV_LORA = 64
RWKV_GN_EPS = 64e-5
S5_GROUP = 16
S5_GROUPS = WIDTH // S5_GROUP
S5_STATE = 64
S5_BLOCK = 16
S5_ROW = S5_BLOCK * S5_GROUP

LANES = 128
VMEM_LIMIT = 48 * 1024 * 1024

C_QA, C_KA, C_VA, C_OA, C_GA = 0, 512, 1024, 2048, 3072
C_QB, C_FB, C_IB, C_GB = 4096, 5120, 7168, 8192
C_RC, C_KC, C_VC, C_GC = 9216, 10240, 11264, 12288
C_UD, C_GD = 13312, 14336
N_MAIN = 15360
N_SMALL = 384
G_OFF = 256


def _params(**kw):
    return pltpu.CompilerParams(vmem_limit_bytes=VMEM_LIMIT, **kw)


def _tile(total, cap, mult):
    best = None
    for t in range(mult, min(total, cap) + 1, mult):
        if total % t == 0:
            best = t
    assert best is not None, (total, cap, mult)
    return best


def _sigmoid(x):
    return jax.nn.sigmoid(x)


def _silu(x):
    return x * jax.nn.sigmoid(x)


def _softplus(x):
    return jnp.maximum(x, 0.0) + jnp.log(1.0 + jnp.exp(-jnp.abs(x)))


def _dot_nt(a, b, precision=None):
    return lax.dot_general(a, b, (((1,), (1,)), ((), ())), precision=precision, preferred_element_type=F32)


def _dot_tn(a, b, precision=None):
    return lax.dot_general(a, b, (((0,), (0,)), ((), ())), precision=precision, preferred_element_type=F32)


def _rmsnorm_body(x_ref, w_ref, o_ref):
    x = x_ref[...]
    y = x * lax.rsqrt(jnp.mean(x * x, axis=-1, keepdims=True) + RMS_EPS) * w_ref[...]
    o_ref[...] = y.astype(o_ref.dtype)


def _rmsnorm(x2d, w, out_dtype):
    m, d = x2d.shape
    tm = _tile(m, 512, 16)
    return pl.pallas_call(
        _rmsnorm_body,
        grid=(m // tm,),
        in_specs=[pl.BlockSpec((tm, d), lambda i: (i, 0)), pl.BlockSpec((1, d), lambda i: (0, 0))],
        out_specs=pl.BlockSpec((tm, d), lambda i: (i, 0)),
        out_shape=jax.ShapeDtypeStruct((m, d), out_dtype),
        compiler_params=_params(dimension_semantics=("parallel",)),
        name="rmsnorm",
    )(x2d, w.reshape(1, d).astype(F32))


def _mm_body(*refs, has_res):
    x_ref, w_ref = refs[0], refs[1]
    o_ref = refs[-1]
    acc = jnp.dot(x_ref[...], w_ref[...], preferred_element_type=F32)
    if has_res:
        acc = refs[2][...] + acc
    o_ref[...] = acc.astype(o_ref.dtype)


def _mm(x, w, res=None, out_dtype=F32, name="mm"):
    m, k = x.shape
    n = w.shape[1]
    tm = _tile(m, 1408, 16)
    tn = 1024 if n % 1024 == 0 else n
    in_specs = [pl.BlockSpec((tm, k), lambda i, j: (i, 0)), pl.BlockSpec((k, tn), lambda i, j: (0, j))]
    args = [x, w]
    if res is not None:
        in_specs.append(pl.BlockSpec((tm, tn), lambda i, j: (i, j)))
        args.append(res)
    return pl.pallas_call(
        functools.partial(_mm_body, has_res=res is not None),
        grid=(m // tm, n // tn),
        in_specs=in_specs,
        out_specs=pl.BlockSpec((tm, tn), lambda i, j: (i, j)),
        out_shape=jax.ShapeDtypeStruct((m, n), out_dtype),
        compiler_params=_params(dimension_semantics=("parallel", "parallel")),
        name=name,
    )(*args)


def _mlstm_body(q_ref, k_ref, v_ref, zs_ref, bias_ref, o_ref, c_sc, n_sc, m_sc, *, nc, lreal):
    d = pl.program_id(1)
    c = pl.program_id(2)
    cc = c + d * (nc - 1 - 2 * c)
    rev = d == 1

    @pl.when(c == 0)
    def _():
        c_sc[...] = jnp.zeros_like(c_sc)
        n_sc[...] = jnp.zeros_like(n_sc)
        m_sc[...] = jnp.zeros_like(m_sc)

    L = CHUNK
    row = lax.broadcasted_iota(jnp.int32, (L, 1), 0)
    valid = (cc * L + row) < lreal
    lane = lax.broadcasted_iota(jnp.int32, (1, LANES), 1)
    g = zs_ref[0][:, G_OFF:G_OFF + LANES] + bias_ref[:, G_OFF:G_OFF + LANES]
    is_f = (lane >= 2 * MLSTM_HEADS) & (lane < 4 * MLSTM_HEADS)
    lf_all = jnp.where(valid & is_f, -_softplus(-g), 0.0)
    r_i = lax.broadcasted_iota(jnp.int32, (L, L), 0)
    c_i = lax.broadcasted_iota(jnp.int32, (L, L), 1)
    sgn = 1 - 2 * d
    causal = (c_i - r_i) * sgn <= 0
    b_all = jnp.dot(causal.astype(F32), lf_all, precision=HI, preferred_element_type=F32)

    q = q_ref[0]
    k = k_ref[0] * (MLSTM_DQK ** -0.5)
    v = v_ref[0]
    for h in range(MLSTM_HEADS):
        fcol = 2 * MLSTM_HEADS + MLSTM_HEADS * d + h
        icol = MLSTM_HEADS * d + h
        b = jnp.sum(jnp.where(lane == fcol, b_all, 0.0), axis=1, keepdims=True)
        li = jnp.sum(jnp.where(lane == icol, g, 0.0), axis=1, keepdims=True)
        li = jnp.where(valid, li, NEG)
        fm = jnp.where(lane == 0, b, jnp.where(lane == 1, 1.0, 0.0))
        gm = jnp.where(lane == 0, 1.0, jnp.where(lane == 1, li - b, 0.0))
        dlog = jnp.where(causal, _dot_nt(fm, gm, HI), NEG)
        m_st = m_sc[h][:, 0:1]
        a = b + m_st
        mt = jnp.maximum(a, jnp.max(dlog, axis=1, keepdims=True))
        qh = q[:, h * MLSTM_DQK:(h + 1) * MLSTM_DQK]
        kh = k[:, h * MLSTM_DQK:(h + 1) * MLSTM_DQK]
        vh = v[:, h * MLSTM_DV:(h + 1) * MLSTM_DV]
        w = jnp.exp(dlog - mt) * _dot_nt(qh, kh, HI)
        inter = jnp.exp(a - mt)
        num = inter * jnp.dot(qh, c_sc[h], precision=HI, preferred_element_type=F32) \
            + jnp.dot(w, vh, precision=HI, preferred_element_type=F32)
        den = inter * jnp.sum(qh * n_sc[h], axis=1, keepdims=True) + jnp.sum(w, axis=1, keepdims=True)
        o_ref[0, 0, :, h * MLSTM_DV:(h + 1) * MLSTM_DV] = num / jnp.maximum(jnp.abs(den), jnp.exp(-mt))
        bl = jnp.where(rev, b[0:1], b[L - 1:L])
        tail = bl - b + li
        m_new = jnp.maximum(bl + m_st, jnp.max(tail, axis=0, keepdims=True))
        kw = kh * jnp.exp(tail - m_new)
        decay = jnp.exp(bl + m_st - m_new)
        c_sc[h] = decay * c_sc[h] + _dot_tn(kw, vh, HI)
        n_sc[h] = decay * n_sc[h] + jnp.sum(kw, axis=0, keepdims=True)
        m_sc[h] = jnp.broadcast_to(m_new, (1, LANES))


def _mlstm_scan(zm, zs, small_bias, lreal):
    n, tp, _ = zm.shape
    nc = tp // CHUNK
    cidx = lambda c, d: c + d * (nc - 1 - 2 * c)
    return pl.pallas_call(
        functools.partial(_mlstm_body, nc=nc, lreal=lreal),
        grid=(n, 2, nc),
        in_specs=[
            pl.BlockSpec((1, CHUNK, 512), lambda i, d, c: (i, cidx(c, d), C_QA // 512)),
            pl.BlockSpec((1, CHUNK, 512), lambda i, d, c: (i, cidx(c, d), C_KA // 512)),
            pl.BlockSpec((1, CHUNK, WIDTH), lambda i, d, c: (i, cidx(c, d), C_VA // WIDTH)),
            pl.BlockSpec((1, CHUNK, N_SMALL), lambda i, d, c: (i, cidx(c, d), 0)),
            pl.BlockSpec((1, N_SMALL), lambda i, d, c: (0, 0)),
        ],
        out_specs=pl.BlockSpec((1, 1, CHUNK, WIDTH), lambda i, d, c: (i, d, cidx(c, d), 0)),
        out_shape=jax.ShapeDtypeStruct((n, 2, tp, WIDTH), F32),
        scratch_shapes=[
            pltpu.VMEM((MLSTM_HEADS, MLSTM_DQK, MLSTM_DV), F32),
            pltpu.VMEM((MLSTM_HEADS, 1, MLSTM_DQK), F32),
            pltpu.VMEM((MLSTM_HEADS, 1, LANES), F32),
        ],
        compiler_params=_params(dimension_semantics=("parallel", "parallel", "arbitrary")),
        name="mlstm_scan",
    )(zm, zm, zm, zs, small_bias)


def _mlstm_post_body(h_ref, oa_ref, ga_ref, nw_ref, o_ref):
    h = h_ref[0, 0] + h_ref[0, 1]
    for j in range(MLSTM_HEADS):
        sl = slice(j * MLSTM_DV, (j + 1) * MLSTM_DV)
        hh = h[:, sl]
        hn = hh * lax.rsqrt(jnp.mean(hh * hh, axis=1, keepdims=True) + RMS_EPS) * nw_ref[:, sl]
        o_ref[0, :, sl] = (hn * _sigmoid(oa_ref[0][:, sl]) * _silu(ga_ref[0][:, sl])).astype(o_ref.dtype)


def _mlstm_post(hd, zm, norm_w):
    n, _, tp, _ = hd.shape
    tr = _tile(tp, 512, 16)
    return pl.pallas_call(
        _mlstm_post_body,
        grid=(n, tp // tr),
        in_specs=[
            pl.BlockSpec((1, 2, tr, WIDTH), lambda i, r: (i, 0, r, 0)),
            pl.BlockSpec((1, tr, WIDTH), lambda i, r: (i, r, C_OA // WIDTH)),
            pl.BlockSpec((1, tr, WIDTH), lambda i, r: (i, r, C_GA // WIDTH)),
            pl.BlockSpec((1, WIDTH), lambda i, r: (0, 0)),
        ],
        out_specs=pl.BlockSpec((1, tr, WIDTH), lambda i, r: (i, r, 0)),
        out_shape=jax.ShapeDtypeStruct((n, tp, WIDTH), BF16),
        compiler_params=_params(dimension_semantics=("parallel", "parallel")),
        name="mlstm_post",
    )(hd, zm, zm, norm_w.reshape(1, WIDTH))


def _cumsum_rows(tri_bf16, x):
    hi = x.astype(BF16)
    r1 = x - hi.astype(F32)
    mid = r1.astype(BF16)
    lo = (r1 - mid.astype(F32)).astype(BF16)
    dot = lambda t: jnp.dot(tri_bf16, t, preferred_element_type=F32)
    return dot(hi) + dot(mid) + dot(lo)


def _hgrn_body(q_ref, f_ref, v_ref, lb_ref, o_ref, st_sc, *, nc, lreal, rev):
    c = pl.program_id(1)
    cc = nc - 1 - c if rev else c

    @pl.when(c == 0)
    def _():
        st_sc[...] = jnp.zeros_like(st_sc)

    L = CHUNK
    sub = 8
    row = lax.broadcasted_iota(jnp.int32, (L, 1), 0)
    valid = (cc * L + row) < lreal
    r_i = lax.broadcasted_iota(jnp.int32, (L, L), 0)
    c_i = lax.broadcasted_iota(jnp.int32, (L, L), 1)
    tri = ((c_i >= r_i) if rev else (c_i <= r_i)).astype(BF16)
    ones = jnp.ones((HGRN_EXPAND, HGRN_EXPAND), BF16)

    def head(h, carry):
        sl = pl.ds(pl.multiple_of(h * HGRN_EXPAND, HGRN_EXPAND), HGRN_EXPAND)
        q = _silu(q_ref[0, :, sl])
        fpre = f_ref[0, :, sl]
        v = v_ref[0, :, sl]
        lb = lb_ref[0, :, sl]
        log_f = jnp.where(valid, jnp.log(lb + (1.0 - lb) * _sigmoid(fpre)), 0.0)
        kk = jnp.where(valid, (1.0 - lb) * _sigmoid(-fpre), 0.0)
        b = _cumsum_rows(tri, log_f)
        acc = [jnp.zeros((sub, HGRN_EXPAND), F32) for _ in range(L // sub)]
        for j0 in range(L // sub):
            lo_v, hi_v = (0, j0 + 1) if rev else (j0, L // sub)
            nr = (hi_v - lo_v) * sub
            rows = slice(lo_v * sub, hi_v * sub)
            pieces = []
            for s in range(j0 * sub, (j0 + 1) * sub):
                dlt = b[rows] - b[s:s + 1]
                seen = (row[rows] <= s) if rev else (row[rows] >= s)
                pieces.append((q[rows] * kk[s:s + 1]) * jnp.exp(jnp.where(seen, dlt, NEG)))
            red = jnp.dot(jnp.concatenate(pieces, axis=0).astype(BF16), ones, preferred_element_type=F32)
            for idx in range(sub):
                s = j0 * sub + idx
                for jj in range(hi_v - lo_v):
                    r0 = idx * nr + jj * sub
                    acc[lo_v + jj] = acc[lo_v + jj] + red[r0:r0 + sub] * v[s:s + 1]
        bl = b[0:1] if rev else b[L - 1:L]
        st = st_sc[h]
        o_ref[0, :, sl] = jnp.concatenate(acc, axis=0) + _dot_nt((q * jnp.exp(b)).astype(BF16), st.astype(BF16))
        st_sc[h] = st * jnp.exp(bl) + _dot_tn(v, kk * jnp.exp(bl - b))
        return carry

    lax.fori_loop(0, HGRN_HEADS, head, 0)


def _hgrn_scan(zm, lb, lreal, rev):
    n, tp, _ = zm.shape
    nc = tp // CHUNK
    cidx = (lambda c: nc - 1 - c) if rev else (lambda c: c)
    return pl.pallas_call(
        functools.partial(_hgrn_body, nc=nc, lreal=lreal, rev=rev),
        grid=(n, nc),
        in_specs=[
            pl.BlockSpec((1, CHUNK, WIDTH), lambda i, c: (i, cidx(c), C_QB // WIDTH)),
            pl.BlockSpec((1, CHUNK, WIDTH), lambda i, c: (i, cidx(c), C_FB // WIDTH + int(rev))),
            pl.BlockSpec((1, CHUNK, WIDTH), lambda i, c: (i, cidx(c), C_IB // WIDTH)),
            pl.BlockSpec((1, 1, WIDTH), lambda i, c: (int(rev), 0, 0)),
        ],
        out_specs=pl.BlockSpec((1, CHUNK, WIDTH), lambda i, c: (i, cidx(c), 0)),
        out_shape=jax.ShapeDtypeStruct((n, tp, WIDTH), F32),
        scratch_shapes=[pltpu.VMEM((HGRN_HEADS, HGRN_EXPAND, HGRN_EXPAND), F32)],
        compiler_params=_params(dimension_semantics=("parallel", "arbitrary")),
        name="hgrn_scan_bwd" if rev else "hgrn_scan_fwd",
    )(zm, zm, zm, lb)


def _hgrn_post_body(of_ref, ob_ref, gb_ref, nw_ref, y_ref):
    o = of_ref[0] + ob_ref[0]
    for j in range(HGRN_HEADS):
        sl = slice(j * HGRN_EXPAND, (j + 1) * HGRN_EXPAND)
        oh = o[:, sl]
        on = oh * lax.rsqrt(jnp.mean(oh * oh, axis=1, keepdims=True) + RMS_EPS) * nw_ref[...]
        y_ref[0, :, sl] = (on * _silu(gb_ref[0][:, sl])).astype(y_ref.dtype)


def _hgrn_post(of, ob, zm, norm_w):
    n, tp, _ = of.shape
    tr = _tile(tp, 512, 16)
    spec = pl.BlockSpec((1, tr, WIDTH), lambda i, r: (i, r, 0))
    return pl.pallas_call(
        _hgrn_post_body,
        grid=(n, tp // tr),
        in_specs=[spec, spec,
                  pl.BlockSpec((1, tr, WIDTH), lambda i, r: (i, r, C_GB // WIDTH)),
                  pl.BlockSpec((1, HGRN_EXPAND), lambda i, r: (0, 0))],
        out_specs=spec,
        out_shape=jax.ShapeDtypeStruct((n, tp, WIDTH), BF16),
        compiler_params=_params(dimension_semantics=("parallel", "parallel")),
        name="hgrn_post",
    )(of, ob, zm, norm_w.reshape(1, HGRN_EXPAND))


def _shift_body(x_ref, mu_ref, o_ref, *, lreal):
    x = x_ref[0]
    tp = x.shape[0]
    row = lax.broadcasted_iota(jnp.int32, (tp, 1), 0)
    prev = jnp.where(row == 0, 0.0, pltpu.roll(x, 1, 0))
    nxt = jnp.where(row + 1 >= lreal, 0.0, pltpu.roll(x, tp - 1, 0))
    o_ref[0] = x + mu_ref[0:1, :] * (prev - x) + mu_ref[1:2, :] * (nxt - x)


def _token_shift(z, col0, ncols, mu, lreal):
    n, tp, _ = z.shape
    tc = LANES
    return pl.pallas_call(
        functools.partial(_shift_body, lreal=lreal),
        grid=(n, ncols // tc),
        in_specs=[
            pl.BlockSpec((1, tp, tc), lambda i, j: (i, 0, col0 // tc + j)),
            pl.BlockSpec((2, tc), lambda i, j: (0, j)),
        ],
        out_specs=pl.BlockSpec((1, tp, tc), lambda i, j: (i, 0, j)),
        out_shape=jax.ShapeDtypeStruct((n, tp, ncols), F32),
        compiler_params=_params(dimension_semantics=("parallel", "parallel")),
        name="token_shift",
    )(z, mu)


def _seg_ones():
    r = lax.broadcasted_iota(jnp.int32, (LANES, LANES), 0) // RWKV_HEAD
    c = lax.broadcasted_iota(jnp.int32, (LANES, LANES), 1) // RWKV_HEAD
    return (r == c).astype(F32)


def _head_sum(x, seg):
    parts = [jnp.dot(x[:, j * LANES:(j + 1) * LANES], seg, precision=HI, preferred_element_type=F32)
             for j in range(WIDTH // LANES)]
    return jnp.concatenate(parts, axis=1)


def _rwkv_pre_body(f_ref, fs_ref, w2_ref, a2_ref, w0_ref, a0_ref, kk_ref, ka_ref, rk_ref,
                   r_o, v_o, kk_o, wf_o, wb_o, nf_o, nb_o, kf_o, kb_o, bonus_o, *, lreal):
    f = f_ref[0]
    r = f[:, 0:WIDTH]
    k = f[:, WIDTH:2 * WIDTH]
    v = f[:, 2 * WIDTH:3 * WIDTH]
    fs = fs_ref[0]
    wl = fs[:, 0:LANES]
    al = fs[:, LANES:2 * LANES]
    seg = _seg_ones()
    w = w0_ref[...] + jnp.dot(jnp.tanh(wl), w2_ref[...], precision=HI, preferred_element_type=F32)
    wdec = jnp.exp(-jnp.exp(-_softplus(-w) - 0.5))
    a = _sigmoid(a0_ref[...] + jnp.dot(al, a2_ref[...], precision=HI, preferred_element_type=F32))
    kk = k * kk_ref[...]
    kk = kk / jnp.maximum(jnp.sqrt(_head_sum(kk * kk, seg)), 1e-12)
    a_f, a_b = a[:, 0:WIDTH], a[:, WIDTH:2 * WIDTH]
    kd_f = k * (1.0 + (a_f - 1.0) * ka_ref[...])
    kd_b = k * (1.0 + (a_b - 1.0) * ka_ref[...])
    tr = f.shape[0]
    valid = (pl.program_id(1) * tr + lax.broadcasted_iota(jnp.int32, (tr, 1), 0)) < lreal
    keep = lambda x: jnp.where(valid, x, 0.0)

    def put(o_ref, x):
        for p in range(WIDTH // LANES):
            o_ref[0, :, p, :] = x[:, p * LANES:(p + 1) * LANES]

    put(r_o, keep(r))
    put(v_o, keep(v))
    put(kk_o, keep(kk))
    put(wf_o, jnp.where(valid, wdec[:, 0:WIDTH], 1.0))
    put(wb_o, jnp.where(valid, wdec[:, WIDTH:2 * WIDTH], 1.0))
    put(nf_o, keep(-(kk * a_f)))
    put(nb_o, keep(-(kk * a_b)))
    put(kf_o, keep(kd_f))
    put(kb_o, keep(kd_b))
    bonus_o[0] = _head_sum(r * (kd_f + kd_b) * rk_ref[...], seg) * v


def _rwkv_pre(fsh, fsm, w2cat, a2cat, w0, a0, k_k, k_a, r_k, lreal):
    n, tp, _ = fsh.shape
    tr = _tile(tp, 256, 8)
    row = lambda width: pl.BlockSpec((1, width), lambda i, r: (0, 0))
    out_spec = pl.BlockSpec((1, tr, WIDTH), lambda i, r: (i, r, 0))
    return pl.pallas_call(
        functools.partial(_rwkv_pre_body, lreal=lreal),
        grid=(n, tp // tr),
        in_specs=[
            pl.BlockSpec((1, tr, 3 * WIDTH), lambda i, r: (i, r, 0)),
            pl.BlockSpec((1, tr, 2 * LANES), lambda i, r: (i, r, 0)),
            pl.BlockSpec((LANES, 2 * WIDTH), lambda i, r: (0, 0)),
            pl.BlockSpec((LANES, 2 * WIDTH), lambda i, r: (0, 0)),
            row(2 * WIDTH), row(2 * WIDTH), row(WIDTH), row(WIDTH), row(WIDTH),
        ],
        out_specs=[pl.BlockSpec((1, tr, WIDTH // LANES, LANES), lambda i, r: (i, r, 0, 0))] * 9 + [out_spec],
        out_shape=[jax.ShapeDtypeStruct((n, tp, WIDTH // LANES, LANES), F32)] * 9
        + [jax.ShapeDtypeStruct((n, tp, WIDTH), F32)],
        compiler_params=_params(dimension_semantics=("parallel", "parallel")),
        name="rwkv_pre",
    )(fsh, fsm, w2cat, a2cat, w0, a0, k_k, k_a, r_k)


RWKV_PAIRS = WIDTH // LANES
RWKV_TB = 16
RWKV_NV1 = 8


def _rwkv_chain_step(s_sc, t_sc, v_g, g, nv):
    base = g * RWKV_HEAD

    def p1(kq, acc):
        for u in range(4):
            kx = kq * 4 + u
            acc = acc + s_sc[g, kx] * t_sc[1, pl.ds(base + kx, 1), :]
        return acc

    sa = lax.fori_loop(0, RWKV_HEAD // 4, p1, jnp.zeros((nv, LANES), F32))

    def p2(kq, acc):
        for u in range(4):
            kx = kq * 4 + u
            s = (s_sc[g, kx] * t_sc[0, pl.ds(base + kx, 1), :]
                 + sa * t_sc[2, pl.ds(base + kx, 1), :]
                 + v_g * t_sc[3, pl.ds(base + kx, 1), :])
            s_sc[g, kx] = s
            acc = acc + s * t_sc[4, pl.ds(base + kx, 1), :]
        return acc

    return lax.fori_loop(0, RWKV_HEAD // 4, p2, jnp.zeros((nv, LANES), F32))


def _rwkv_scan8_body(wf, wb, kkf, kkb, nf, nb, kdf, kdb, rf, rb, vf, vb, yf_ref, yb_ref, s_sc, t_sc):
    @pl.when(pl.program_id(0) == 0)
    def _():
        s_sc[...] = jnp.zeros_like(s_sc)

    nseq = wf.shape[0]

    def step(j, carry):
        jb = RWKV_TB - 1 - j

        def tile(xf_ref, xb_ref):
            rows = [xf_ref[i, j] for i in range(nseq)] + [xb_ref[i, jb] for i in range(nseq)]
            return jnp.concatenate(rows, axis=0).T

        for i, (a, b) in enumerate(((wf, wb), (kkf, kkb), (nf, nb), (kdf, kdb), (rf, rb))):
            t_sc[i] = tile(a, b)
        vt = tile(vf, vb)
        ys = [_rwkv_chain_step(s_sc, t_sc, vt[g * RWKV_HEAD:(g + 1) * RWKV_HEAD], g, RWKV_HEAD) for g in range(2)]
        yt = jnp.concatenate(ys, axis=0).T
        for i in range(nseq):
            yf_ref[i, j] = yt[i * RWKV_PAIRS:(i + 1) * RWKV_PAIRS]
            yb_ref[i, jb] = yt[(nseq + i) * RWKV_PAIRS:(nseq + i + 1) * RWKV_PAIRS]
        return carry

    lax.fori_loop(0, RWKV_TB, step, 0)


def _rwkv_scan1_body(wf, wb, kkf, kkb, nf, nb, kdf, kdb, rf, rb, vf, vb, yf_ref, yb_ref, s_sc, t_sc):
    @pl.when(pl.program_id(0) == 0)
    def _():
        s_sc[...] = jnp.zeros_like(s_sc)

    nv = RWKV_NV1

    def step(j, carry):
        jb = RWKV_TB - 1 - j

        def rows8(xf_ref, xb_ref):
            out = [jnp.broadcast_to(xf_ref[0, j, p:p + 1, :], (8, LANES)) for p in range(RWKV_PAIRS)]
            out += [jnp.broadcast_to(xb_ref[0, jb, p:p + 1, :], (8, LANES)) for p in range(RWKV_PAIRS)]
            return out

        for i, (a, b) in enumerate(((wf, wb), (kkf, kkb), (nf, nb), (kdf, kdb), (rf, rb))):
            t_sc[i] = jnp.concatenate(rows8(a, b), axis=0).T
        vt = jnp.concatenate(rows8(vf, vb), axis=0).T
        octet = lax.broadcasted_iota(jnp.int32, (nv, LANES), 1) % (RWKV_HEAD // nv)
        ys = []
        for g in range(2):
            v_g = jnp.zeros((nv, LANES), F32)
            for q in range(RWKV_HEAD // nv):
                r0 = g * RWKV_HEAD + q * nv
                v_g = jnp.where(octet == q, vt[r0:r0 + nv], v_g)
            ys.append(_rwkv_chain_step(s_sc, t_sc, v_g, g, nv))
        zrows = [jnp.where(octet == q, ys[g], 0.0) for g in range(2) for q in range(RWKV_HEAD // nv)]
        yt = jnp.concatenate(zrows, axis=0).T
        for dp in range(2 * RWKV_PAIRS):
            rowv = jnp.sum(yt[dp * 8:(dp + 1) * 8], axis=0, keepdims=True)
            p = dp % RWKV_PAIRS
            if dp < RWKV_PAIRS:
                yf_ref[0, j, p:p + 1, :] = rowv
            else:
                yb_ref[0, jb, p:p + 1, :] = rowv
        return carry

    lax.fori_loop(0, RWKV_TB, step, 0)


def _rwkv_scan(wf, wb, kk, nf, nb, kdf, kdb, r, v):
    n, tp = r.shape[:2]
    nblk = tp // RWKV_TB
    fspec = pl.BlockSpec((n, RWKV_TB, RWKV_PAIRS, LANES), lambda i: (0, i, 0, 0))
    bspec = pl.BlockSpec((n, RWKV_TB, RWKV_PAIRS, LANES), lambda i: (0, nblk - 1 - i, 0, 0))
    if n == 8:
        body, nv = _rwkv_scan8_body, RWKV_HEAD
    else:
        assert n == 1
        body, nv = _rwkv_scan1_body, RWKV_NV1
    return pl.pallas_call(
        body,
        grid=(nblk,),
        in_specs=[fspec, bspec] * 6,
        out_specs=[fspec, bspec],
        out_shape=[jax.ShapeDtypeStruct((n, tp, RWKV_PAIRS, LANES), F32)] * 2,
        scratch_shapes=[pltpu.VMEM((2, RWKV_HEAD, nv, LANES), F32), pltpu.VMEM((5, LANES, LANES), F32)],
        compiler_params=_params(dimension_semantics=("arbitrary",)),
        name="rwkv_scan",
    )(wf, wb, kk, kk, nf, nb, kdf, kdb, r, r, v, v)


def _rwkv_post_body(yf_ref, yb_ref, bonus_ref, gc_ref, lnw_ref, lnb_ref, o_ref):
    seg = _seg_ones()
    y = jnp.concatenate([yf_ref[0, :, p, :] + yb_ref[0, :, p, :] for p in range(RWKV_PAIRS)], axis=1)
    mean = _head_sum(y, seg) * (1.0 / RWKV_HEAD)
    yc = y - mean
    var = _head_sum(yc * yc, seg) * (1.0 / RWKV_HEAD)
    yn = yc * lax.rsqrt(var + RWKV_GN_EPS) * lnw_ref[...] + lnb_ref[...]
    o_ref[0] = ((yn + bonus_ref[0]) * _silu(gc_ref[0])).astype(o_ref.dtype)


def _rwkv_post(yf, yb, bonus, zm, ln_w, ln_b):
    n, tp = yf.shape[:2]
    tr = _tile(tp, 512, 16)
    spec = pl.BlockSpec((1, tr, WIDTH), lambda i, r: (i, r, 0))
    yspec = pl.BlockSpec((1, tr, RWKV_PAIRS, LANES), lambda i, r: (i, r, 0, 0))
    row = pl.BlockSpec((1, WIDTH), lambda i, r: (0, 0))
    return pl.pallas_call(
        _rwkv_post_body,
        grid=(n, tp // tr),
        in_specs=[yspec, yspec, spec, pl.BlockSpec((1, tr, WIDTH), lambda i, r: (i, r, C_GC // WIDTH)), row, row],
        out_specs=spec,
        out_shape=jax.ShapeDtypeStruct((n, tp, WIDTH), BF16),
        compiler_params=_params(dimension_semantics=("parallel", "parallel")),
        name="rwkv_post",
    )(yf, yb, bonus, zm, ln_w.reshape(1, WIDTH), ln_b.reshape(1, WIDTH))


def _rwkv_branch(zm, zs, p, lreal):
    fsh = _token_shift(zm, C_RC, 3 * WIDTH, p['mu_main'], lreal)
    fsm = _token_shift(zs, 0, 2 * LANES, p['mu_small'], lreal)
    r, v, kk, wf, wb, nf, nb, kf, kb, bonus = _rwkv_pre(
        fsh, fsm, p['w2cat'], p['a2cat'], p['w0'], p['a0'], p['k_k'], p['k_a'], p['r_k'], lreal)
    yf, yb = _rwkv_scan(wf, wb, kk, nf, nb, kf, kb, r, v)
    return _rwkv_post(yf, yb, bonus, zm, p['ln_w'], p['ln_b'])


def _s5_core_body(u_ref, m_ref, n_ref, q_ref, pw_ref, cidx_ref, y_ref, *, nsteps, nvalid, nblocks):
    cidx = cidx_ref[...]
    u = jnp.where(cidx < nvalid, u_ref[0], 0.0)
    rows = u.shape[0]
    y = None
    for d in range(2):
        yd = jnp.dot(u, m_ref[d, 0], precision=HI, preferred_element_type=F32)
        x = jnp.dot(u, n_ref[d, 0], precision=HI, preferred_element_type=F32)
        for j in range(nsteps):
            s = 1 << j
            if d == 0:
                xs = jnp.where(cidx >= s, pltpu.roll(x, s, 0), 0.0)
            else:
                xs = jnp.where(cidx + s < nblocks, pltpu.roll(x, rows - s, 0), 0.0)
            x = (x + xs * pw_ref[d, 0, 2 * j:2 * j + 1, :]
                 + pltpu.roll(xs, S5_STATE, 1) * pw_ref[d, 0, 2 * j + 1:2 * j + 2, :])
        if d == 0:
            xin = jnp.where(cidx >= 1, pltpu.roll(x, 1, 0), 0.0)
        else:
            xin = jnp.where(cidx + 1 < nblocks, pltpu.roll(x, rows - 1, 0), 0.0)
        yd = yd + jnp.dot(xin, q_ref[d, 0], precision=HI, preferred_element_type=F32)
        y = yd if y is None else y + yd
    y_ref[0] = y


def _s5_core(u, mats, cidx, nsteps, nvalid, nblocks):
    g, rows, _ = u.shape
    mm, nn, qq, pw = mats
    return pl.pallas_call(
        functools.partial(_s5_core_body, nsteps=nsteps, nvalid=nvalid, nblocks=nblocks),
        grid=(g,),
        in_specs=[
            pl.BlockSpec((1, rows, S5_ROW), lambda j: (j, 0, 0)),
            pl.BlockSpec((2, 1, S5_ROW, S5_ROW), lambda j: (0, j, 0, 0)),
            pl.BlockSpec((2, 1, S5_ROW, 2 * S5_STATE), lambda j: (0, j, 0, 0)),
            pl.BlockSpec((2, 1, 2 * S5_STATE, S5_ROW), lambda j: (0, j, 0, 0)),
            pl.BlockSpec((2, 1, pw.shape[2], 2 * S5_STATE), lambda j: (0, j, 0, 0)),
            pl.BlockSpec((rows, 1), lambda j: (0, 0)),
        ],
        out_specs=pl.BlockSpec((1, rows, S5_ROW), lambda j: (j, 0, 0)),
        out_shape=jax.ShapeDtypeStruct((g, rows, S5_ROW), F32),
        compiler_params=_params(dimension_semantics=("parallel",)),
        name="s5_core",
    )(u, mm, nn, qq, pw, cidx)


def _s5_post_body(y_ref, ud_ref, gd_ref, d_ref, w_ref, b_ref, o_ref):
    y = y_ref[0] + d_ref[...] * ud_ref[0]
    g = 0.5 * y * (1.0 + jnp.tanh(0.7978845608028654 * (y + 0.044715 * (y * y * y))))
    glu = jnp.dot(g.astype(BF16), w_ref[...], preferred_element_type=F32) + b_ref[...]
    o_ref[0] = (g * _sigmoid(glu) * _silu(gd_ref[0])).astype(o_ref.dtype)


def _s5_post(ys, zm, d, glu_w, glu_b):
    n, tp, _ = ys.shape
    tr = _tile(tp, 512, 16)
    spec = pl.BlockSpec((1, tr, WIDTH), lambda i, r: (i, r, 0))
    row = pl.BlockSpec((1, WIDTH), lambda i, r: (0, 0))
    return pl.pallas_call(
        _s5_post_body,
        grid=(n, tp // tr),
        in_specs=[spec,
                  pl.BlockSpec((1, tr, WIDTH), lambda i, r: (i, r, C_UD // WIDTH)),
                  pl.BlockSpec((1, tr, WIDTH), lambda i, r: (i, r, C_GD // WIDTH)),
                  row, pl.BlockSpec((WIDTH, WIDTH), lambda i, r: (0, 0)), row],
        out_specs=spec,
        out_shape=jax.ShapeDtypeStruct((n, tp, WIDTH), BF16),
        compiler_params=_params(dimension_semantics=("parallel", "parallel")),
        name="s5_post",
    )(ys, zm, zm, d.reshape(1, WIDTH), glu_w, glu_b.reshape(1, WIDTH))


def _s5_mats(a_re, a_im, log_dt, b_re, b_im, c_re, c_im, max_blocks):
    dt = jnp.exp(log_dt)[..., None]
    mag = jnp.exp(a_re * dt)
    ang = a_im * dt
    ab_re, ab_im = mag * jnp.cos(ang), mag * jnp.sin(ang)
    den = a_re * a_re + a_im * a_im
    xr, yi = ab_re - 1.0, ab_im
    coef_re = (xr * a_re + yi * a_im) / den
    coef_im = (yi * a_re - xr * a_im) / den
    bb_re = coef_re[..., None] * b_re - coef_im[..., None] * b_im
    bb_im = coef_re[..., None] * b_im + coef_im[..., None] * b_re
    pr, pi = [jnp.ones_like(ab_re)], [jnp.zeros_like(ab_re)]
    for _ in range(S5_BLOCK):
        pr, pi = pr + [pr[-1] * ab_re - pi[-1] * ab_im], pi + [pr[-1] * ab_im + pi[-1] * ab_re]
    pw_re, pw_im = jnp.stack(pr), jnp.stack(pi)
    t_re = pw_re[..., None] * bb_re - pw_im[..., None] * bb_im
    t_im = pw_re[..., None] * bb_im + pw_im[..., None] * bb_re
    kj = (jnp.einsum('dgcp,jdgpe->jdgce', c_re, t_re, precision=HI)
          - jnp.einsum('dgcp,jdgpe->jdgce', c_im, t_im, precision=HI))
    s_i = jnp.arange(S5_BLOCK)[:, None]
    i_i = jnp.arange(S5_BLOCK)[None, :]
    rev = jnp.arange(S5_BLOCK - 1, -1, -1)

    def tables(d):
        dist = (s_i - i_i) if d else (i_i - s_i)
        m6 = jnp.where((dist >= 0)[:, :, None, None, None], kj[jnp.clip(dist, 0, S5_BLOCK), d], 0.0)
        mm_d = jnp.transpose(m6, (2, 0, 4, 1, 3)).reshape(S5_GROUPS, S5_ROW, S5_ROW)
        order = jnp.arange(S5_BLOCK) if d else rev
        n_re = jnp.transpose(t_re[order, d], (1, 0, 3, 2)).reshape(S5_GROUPS, S5_ROW, S5_STATE)
        n_im = jnp.transpose(t_im[order, d], (1, 0, 3, 2)).reshape(S5_GROUPS, S5_ROW, S5_STATE)
        steps_in = (rev + 1) if d else (jnp.arange(S5_BLOCK) + 1)
        pr_i, pi_i = pw_re[steps_in, d][:, :, None, :], pw_im[steps_in, d][:, :, None, :]
        ca_re = c_re[d][None] * pr_i - c_im[d][None] * pi_i
        ca_im = c_re[d][None] * pi_i + c_im[d][None] * pr_i
        q_re = jnp.transpose(ca_re, (1, 3, 0, 2)).reshape(S5_GROUPS, S5_STATE, S5_ROW)
        q_im = jnp.transpose(-ca_im, (1, 3, 0, 2)).reshape(S5_GROUPS, S5_STATE, S5_ROW)
        return mm_d, jnp.concatenate([n_re, n_im], axis=-1), jnp.concatenate([q_re, q_im], axis=1)

    mm, nn, qq = (jnp.stack(t) for t in zip(tables(0), tables(1)))
    sr, si = pw_re[S5_BLOCK], pw_im[S5_BLOCK]
    rows = []
    steps = 0
    while (1 << steps) < max_blocks:
        rows += [jnp.concatenate([sr, sr], axis=-1), jnp.concatenate([-si, si], axis=-1)]
        sr, si = sr * sr - si * si, 2.0 * sr * si
        steps += 1
    while len(rows) % 8:
        rows.append(jnp.zeros_like(rows[0]))
    pw = jnp.stack(rows, axis=2)
    return (mm, nn, qq, pw), steps


def _s5_branch(zm, p, lreal):
    n, tp, _ = zm.shape
    assert lreal % S5_BLOCK == 0 and tp % S5_BLOCK == 0
    nvalid = lreal // S5_BLOCK
    nb = tp // S5_BLOCK
    rows = n * nb
    rows_p = -(-rows // 8) * 8
    u = zm[:, :, C_UD:C_UD + WIDTH].reshape(n, nb, S5_BLOCK, S5_GROUPS, S5_GROUP)
    u = jnp.transpose(u, (3, 0, 1, 2, 4)).reshape(S5_GROUPS, rows, S5_ROW)
    u = jnp.pad(u, ((0, 0), (0, rows_p - rows), (0, 0)))
    cidx = jnp.pad(jnp.tile(jnp.arange(nb, dtype=jnp.int32), n), (0, rows_p - rows),
                   constant_values=nb).reshape(rows_p, 1)
    mats, nsteps = p['s5_mats'](nb)
    y = _s5_core(u, mats, cidx, nsteps, nvalid, nb)[:, :rows]
    ys = jnp.transpose(y.reshape(S5_GROUPS, n, nb, S5_BLOCK, S5_GROUP), (1, 2, 3, 0, 4)).reshape(n, tp, WIDTH)
    return _s5_post(ys, zm, p['s5_d'], p['glu_w'], p['glu_b'])


def _merge_body(h_ref, ya_ref, yb_ref, yc_ref, yd_ref, wg_ref, bg_ref, wb_ref, o_ref):
    h = h_ref[...]
    acc = None
    for b, y_ref in enumerate((ya_ref, yb_ref, yc_ref, yd_ref)):
        gate = _sigmoid(jnp.dot(h, wg_ref[b], preferred_element_type=F32) + bg_ref[b])
        term = gate * jnp.dot(y_ref[...], wb_ref[b], preferred_element_type=F32)
        acc = term if acc is None else acc + term
    o_ref[...] = acc.astype(o_ref.dtype)


def _merge(h, ys, wg, bg, wb):
    m, d = h.shape
    tm = _tile(m, 1024, 16)
    tn = 256
    yspec = pl.BlockSpec((tm, WIDTH), lambda i, j: (i, 0))
    return pl.pallas_call(
        _merge_body,
        grid=(m // tm, d // tn),
        in_specs=[pl.BlockSpec((tm, d), lambda i, j: (i, 0)), yspec, yspec, yspec, yspec,
                  pl.BlockSpec((4, d, tn), lambda i, j: (0, 0, j)),
                  pl.BlockSpec((4, 1, tn), lambda i, j: (0, 0, j)),
                  pl.BlockSpec((4, WIDTH, tn), lambda i, j: (0, 0, j))],
        out_specs=pl.BlockSpec((tm, tn), lambda i, j: (i, j)),
        out_shape=jax.ShapeDtypeStruct((m, d), BF16),
        compiler_params=_params(dimension_semantics=("parallel", "parallel")),
        name="merge",
    )(h, *ys, wg, bg, wb)


def _layer(xp, p, lreal):
    n, tp, d = xp.shape
    m = n * tp
    x2 = xp.reshape(m, d)
    h = _rmsnorm(x2, p['norm_w'], BF16)
    zm = _mm(h, p['w_main'], name="proj_main").reshape(n, tp, N_MAIN)
    zs = _mm(h, p['w_small'], name="proj_small").reshape(n, tp, N_SMALL)
    ya = _mlstm_post(_mlstm_scan(zm, zs, p['small_bias'], lreal), zm, p['mlstm_norm_w'])
    yb = _hgrn_post(_hgrn_scan(zm, p['hgrn_lb'], lreal, False), _hgrn_scan(zm, p['hgrn_lb'], lreal, True), zm,
                    p['hgrn_norm_w'])
    yc = _rwkv_branch(zm, zs, p, lreal)
    yd = _s5_branch(zm, p, lreal)
    ys = [y.reshape(m, WIDTH) for y in (ya, yb, yc, yd)]
    merged = _merge(h, ys, p['w_gate'], p['b_gate'], p['w_branch'])
    return _mm(merged, p['w_out'], res=x2, name="proj_out").reshape(n, tp, d)


def _trunk(x, meta, layers, final_norm_w):
    n, t, d = x.shape
    lreal = t + N_META
    tp = lreal + TAIL_PAD
    xp = jnp.concatenate([jnp.broadcast_to(meta[None], (n, N_META, d)), x, jnp.zeros((n, TAIL_PAD, d), x.dtype)], axis=1)
    for p in layers:
        xp = _layer(xp, p, lreal)
    y = _rmsnorm(xp.reshape(n * tp, d), final_norm_w, F32).reshape(n, tp, d)
    return y[:, N_META:lreal]


def _layer_params(l, a):
    w_in = a['w_in'][l]
    offs = {}
    acc = 0
    sizes = (512, 512, WIDTH, WIDTH, 8, 8, WIDTH, WIDTH, 2 * WIDTH, WIDTH, WIDTH, WIDTH, WIDTH, WIDTH,
             2 * RWKV_LORA, 2 * RWKV_LORA, WIDTH, WIDTH, WIDTH)
    names = ('qa', 'ka', 'va', 'oa', 'iga', 'fga', 'ga', 'qb', 'fb', 'ib', 'gb', 'rc', 'kc', 'vc', 'wlc', 'alc',
             'gc', 'ud', 'gd')
    for nm, sz in zip(names, sizes):
        offs[nm] = (acc, acc + sz)
        acc += sz
    col = lambda nm: w_in[:, offs[nm][0]:offs[nm][1]]
    main_order = ('qa', 'ka', 'va', 'oa', 'ga', 'qb', 'fb', 'ib', 'gb', 'rc', 'kc', 'vc', 'gc', 'ud', 'gd')
    w_main = jnp.concatenate([col(nm) for nm in main_order], axis=1).astype(BF16)
    w_small = jnp.concatenate([col('wlc'), col('alc'), col('iga'), col('fga'),
                               jnp.zeros((D_MODEL, LANES - 16), F32)], axis=1).astype(BF16)
    small_bias = jnp.concatenate([jnp.zeros((G_OFF,), F32), a['mlstm_ig_b'][l].reshape(-1),
                                  a['mlstm_fg_b'][l].reshape(-1), jnp.zeros((LANES - 16,), F32)]).reshape(1, N_SMALL)
    lbw = jax.nn.softmax(a['hgrn_lower_bounds'].astype(F32), axis=1)
    hgrn_lb = (jnp.cumsum(lbw, axis=1) - lbw[:, :1])[:, l].reshape(2, 1, WIDTH)
    mu = a['rwkv_shift_mu'][l]
    zero = jnp.zeros((RWKV_LORA, WIDTH), F32)
    blockdiag = lambda w2: jnp.concatenate([jnp.concatenate([w2[0], zero], axis=1),
                                            jnp.concatenate([zero, w2[1]], axis=1)], axis=0)
    s5_args = tuple(a[k][l].astype(F32) for k in ('s5_a_re', 's5_a_im', 's5_log_dt', 's5_b_re', 's5_b_im',
                                                  's5_c_re', 's5_c_im'))
    return {
        'norm_w': a['norm_w'][l], 'w_main': w_main, 'w_small': w_small, 'small_bias': small_bias,
        'mlstm_norm_w': a['mlstm_norm_w'][l], 'hgrn_lb': hgrn_lb, 'hgrn_norm_w': a['hgrn_norm_w'][l],
        'mu_main': mu[:, :3 * WIDTH], 'mu_small': mu[:, 3 * WIDTH:],
        'w2cat': blockdiag(a['rwkv_w2'][l]), 'a2cat': blockdiag(a['rwkv_a2'][l]),
        'w0': a['rwkv_w0'][l].reshape(1, 2 * WIDTH), 'a0': a['rwkv_a0'][l].reshape(1, 2 * WIDTH),
        'k_k': a['rwkv_k_k'][l].reshape(1, WIDTH), 'k_a': a['rwkv_k_a'][l].reshape(1, WIDTH),
        'r_k': a['rwkv_r_k'][l].reshape(1, WIDTH),
        'ln_w': a['rwkv_ln_w'][l], 'ln_b': a['rwkv_ln_b'][l],
        's5_mats': functools.partial(_s5_mats, *s5_args),
        's5_d': a['s5_d'][l], 'glu_w': a['s5_glu_w'][l].astype(BF16), 'glu_b': a['s5_glu_b'][l],
        'w_gate': a['w_gate'][l].astype(BF16), 'b_gate': a['b_gate'][l].reshape(4, 1, D_MODEL),
        'w_branch': a['w_branch'][l].astype(BF16), 'w_out': a['w_out'][l].astype(BF16),
    }


def kernel(x_prompt, x_sample, meta_tokens, norm_w, w_in, mlstm_ig_b, mlstm_fg_b, mlstm_norm_w, hgrn_lower_bounds, hgrn_norm_w, rwkv_shift_mu, rwkv_w0, rwkv_w2, rwkv_a0, rwkv_a2, rwkv_k_k, rwkv_k_a, rwkv_r_k, rwkv_ln_w, rwkv_ln_b, s5_a_re, s5_a_im, s5_log_dt, s5_b_re, s5_b_im, s5_c_re, s5_c_im, s5_d, s5_glu_w, s5_glu_b, w_branch, w_gate, b_gate, w_out, final_norm_w):
    a = dict(norm_w=norm_w, w_in=w_in, mlstm_ig_b=mlstm_ig_b, mlstm_fg_b=mlstm_fg_b, mlstm_norm_w=mlstm_norm_w,
             hgrn_lower_bounds=hgrn_lower_bounds, hgrn_norm_w=hgrn_norm_w, rwkv_shift_mu=rwkv_shift_mu,
             rwkv_w0=rwkv_w0, rwkv_w2=rwkv_w2, rwkv_a0=rwkv_a0, rwkv_a2=rwkv_a2, rwkv_k_k=rwkv_k_k,
             rwkv_k_a=rwkv_k_a, rwkv_r_k=rwkv_r_k, rwkv_ln_w=rwkv_ln_w, rwkv_ln_b=rwkv_ln_b, s5_a_re=s5_a_re,
             s5_a_im=s5_a_im, s5_log_dt=s5_log_dt, s5_b_re=s5_b_re, s5_b_im=s5_b_im, s5_c_re=s5_c_re,
             s5_c_im=s5_c_im, s5_d=s5_d, s5_glu_w=s5_glu_w, s5_glu_b=s5_glu_b, w_branch=w_branch, w_gate=w_gate,
             b_gate=b_gate, w_out=w_out)
    layers = [_layer_params(l, a) for l in range(DEPTH)]
    meta = meta_tokens.astype(x_prompt.dtype)
    return (_trunk(x_prompt, meta, layers, final_norm_w), _trunk(x_sample, meta, layers, final_norm_w))
```

```python
import functools

import jax
import jax.numpy as jnp
from jax import lax
from jax.experimental import pallas as pl
from jax.experimental.pallas import tpu as pltpu

F32 = jnp.float32
BF16 = jnp.bfloat16
HI = lax.Precision.HIGHEST

D_MODEL = 2048
DEPTH = 2
N_META = 16
CHUNK = 64
TAIL_PAD = CHUNK - N_META
WIDTH = D_MODEL // 2
RMS_EPS = 1e-6
NEG = -1e30

MLSTM_HEADS = 4
MLSTM_DV = WIDTH // MLSTM_HEADS
MLSTM_DQK = MLSTM_DV // 2
HGRN_EXPAND = 128
HGRN_HEADS = WIDTH // HGRN_EXPAND
RWKV_HEAD = 64
RWKV_HEADS = WIDTH // RWKV_HEAD
RWKV_LORA = 64
RWKV_GN_EPS = 64e-5
S5_GROUP = 16
S5_GROUPS = WIDTH // S5_GROUP
S5_STATE = 64
S5_BLOCK = 16
S5_ROW = S5_BLOCK * S5_GROUP

LANES = 128
VMEM_LIMIT = 48 * 1024 * 1024

C_QA, C_KA, C_VA, C_OA, C_GA = 0, 512, 1024, 2048, 3072
C_QB, C_FB, C_IB, C_GB = 4096, 5120, 7168, 8192
C_RC, C_KC, C_VC, C_GC = 9216, 10240, 11264, 12288
C_UD, C_GD = 13312, 14336
N_MAIN = 15360
N_SMALL = 384
G_OFF = 256


def _params(**kw):
    return pltpu.CompilerParams(vmem_limit_bytes=VMEM_LIMIT, **kw)


def _tile(total, cap, mult):
    best = None
    for t in range(mult, min(total, cap) + 1, mult):
        if total % t == 0:
            best = t
    assert best is not None, (total, cap, mult)
    return best


def _sigmoid(x):
    return jax.nn.sigmoid(x)


def _silu(x):
    return x * jax.nn.sigmoid(x)


def _softplus(x):
    return jnp.maximum(x, 0.0) + jnp.log(1.0 + jnp.exp(-jnp.abs(x)))


def _dot_nt(a, b, precision=None):
    return lax.dot_general(a, b, (((1,), (1,)), ((), ())), precision=precision, preferred_element_type=F32)


def _dot_tn(a, b, precision=None):
    return lax.dot_general(a, b, (((0,), (0,)), ((), ())), precision=precision, preferred_element_type=F32)


def _rmsnorm_body(x_ref, w_ref, o_ref):
    x = x_ref[...]
    y = x * lax.rsqrt(jnp.mean(x * x, axis=-1, keepdims=True) + RMS_EPS) * w_ref[...]
    o_ref[...] = y.astype(o_ref.dtype)


def _rmsnorm(x2d, w, out_dtype):
    m, d = x2d.shape
    tm = _tile(m, 512, 16)
    return pl.pallas_call(
        _rmsnorm_body,
        grid=(m // tm,),
        in_specs=[pl.BlockSpec((tm, d), lambda i: (i, 0)), pl.BlockSpec((1, d), lambda i: (0, 0))],
        out_specs=pl.BlockSpec((tm, d), lambda i: (i, 0)),
        out_shape=jax.ShapeDtypeStruct((m, d), out_dtype),
        compiler_params=_params(dimension_semantics=("parallel",)),
        name="rmsnorm",
    )(x2d, w.reshape(1, d).astype(F32))


def _mm_body(*refs, has_res):
    x_ref, w_ref = refs[0], refs[1]
    o_ref = refs[-1]
    acc = jnp.dot(x_ref[...], w_ref[...], preferred_element_type=F32)
    if has_res:
        acc = refs[2][...] + acc
    o_ref[...] = acc.astype(o_ref.dtype)


def _mm(x, w, res=None, out_dtype=F32, name="mm"):
    m, k = x.shape
    n = w.shape[1]
    tm = _tile(m, 1408, 16)
    tn = 1024 if n % 1024 == 0 else n
    in_specs = [pl.BlockSpec((tm, k), lambda i, j: (i, 0)), pl.BlockSpec((k, tn), lambda i, j: (0, j))]
    args = [x, w]
    if res is not None:
        in_specs.append(pl.BlockSpec((tm, tn), lambda i, j: (i, j)))
        args.append(res)
    return pl.pallas_call(
        functools.partial(_mm_body, has_res=res is not None),
        grid=(m // tm, n // tn),
        in_specs=in_specs,
        out_specs=pl.BlockSpec((tm, tn), lambda i, j: (i, j)),
        out_shape=jax.ShapeDtypeStruct((m, n), out_dtype),
        compiler_params=_params(dimension_semantics=("parallel", "parallel")),
        name=name,
    )(*args)


def _mlstm_body(q_ref, k_ref, v_ref, zs_ref, bias_ref, o_ref, c_sc, n_sc, m_sc, *, nc, lreal):
    d = pl.program_id(1)
    c = pl.program_id(2)
    cc = c + d * (nc - 1 - 2 * c)
    rev = d == 1

    @pl.when(c == 0)
    def _():
        c_sc[...] = jnp.zeros_like(c_sc)
        n_sc[...] = jnp.zeros_like(n_sc)
        m_sc[...] = jnp.zeros_like(m_sc)

    L = CHUNK
    row = lax.broadcasted_iota(jnp.int32, (L, 1), 0)
    valid = (cc * L + row) < lreal
    lane = lax.broadcasted_iota(jnp.int32, (1, LANES), 1)
    g = zs_ref[0][:, G_OFF:G_OFF + LANES] + bias_ref[:, G_OFF:G_OFF + LANES]
    is_f = (lane >= 2 * MLSTM_HEADS) & (lane < 4 * MLSTM_HEADS)
    lf_all = jnp.where(valid & is_f, -_softplus(-g), 0.0)
    r_i = lax.broadcasted_iota(jnp.int32, (L, L), 0)
    c_i = lax.broadcasted_iota(jnp.int32, (L, L), 1)
    sgn = 1 - 2 * d
    causal = (c_i - r_i) * sgn <= 0
    b_all = jnp.dot(causal.astype(F32), lf_all, precision=HI, preferred_element_type=F32)

    q = q_ref[0]
    k = k_ref[0] * (MLSTM_DQK ** -0.5)
    v = v_ref[0]
    for h in range(MLSTM_HEADS):
        fcol = 2 * MLSTM_HEADS + MLSTM_HEADS * d + h
        icol = MLSTM_HEADS * d + h
        b = jnp.sum(jnp.where(lane == fcol, b_all, 0.0), axis=1, keepdims=True)
        li = jnp.sum(jnp.where(lane == icol, g, 0.0), axis=1, keepdims=True)
        li = jnp.where(valid, li, NEG)
        fm = jnp.where(lane == 0, b, jnp.where(lane == 1, 1.0, 0.0))
        gm = jnp.where(lane == 0, 1.0, jnp.where(lane == 1, li - b, 0.0))
        dlog = jnp.where(causal, _dot_nt(fm, gm, HI), NEG)
        m_st = m_sc[h][:, 0:1]
        a = b + m_st
        mt = jnp.maximum(a, jnp.max(dlog, axis=1, keepdims=True))
        qh = q[:, h * MLSTM_DQK:(h + 1) * MLSTM_DQK]
        kh = k[:, h * MLSTM_DQK:(h + 1) * MLSTM_DQK]
        vh = v[:, h * MLSTM_DV:(h + 1) * MLSTM_DV]
        qb, vb = qh.astype(BF16), vh.astype(BF16)
        w = jnp.exp(dlog - mt) * _dot_nt(qb, kh.astype(BF16))
        inter = jnp.exp(a - mt)
        num = inter * jnp.dot(qb, c_sc[h].astype(BF16), preferred_element_type=F32) \
            + jnp.dot(w.astype(BF16), vb, preferred_element_type=F32)
        den = inter * jnp.sum(qh * n_sc[h], axis=1, keepdims=True) + jnp.sum(w, axis=1, keepdims=True)
        o_ref[0, 0, :, h * MLSTM_DV:(h + 1) * MLSTM_DV] = num / jnp.maximum(jnp.abs(den), jnp.exp(-mt))
        bl = jnp.where(rev, b[0:1], b[L - 1:L])
        tail = bl - b + li
        m_new = jnp.maximum(bl + m_st, jnp.max(tail, axis=0, keepdims=True))
        kw = kh * jnp.exp(tail - m_new)
        decay = jnp.exp(bl + m_st - m_new)
        c_sc[h] = decay * c_sc[h] + _dot_tn(kw, vh)
        n_sc[h] = decay * n_sc[h] + jnp.sum(kw, axis=0, keepdims=True)
        m_sc[h] = jnp.broadcast_to(m_new, (1, LANES))


def _mlstm_scan(zm, zs, small_bias, lreal):
    n, tp, _ = zm.shape
    nc = tp // CHUNK
    cidx = lambda c, d: c + d * (nc - 1 - 2 * c)
    return pl.pallas_call(
        functools.partial(_mlstm_body, nc=nc, lreal=lreal),
        grid=(n, 2, nc),
        in_specs=[
            pl.BlockSpec((1, CHUNK, 512), lambda i, d, c: (i, cidx(c, d), C_QA // 512)),
            pl.BlockSpec((1, CHUNK, 512), lambda i, d, c: (i, cidx(c, d), C_KA // 512)),
            pl.BlockSpec((1, CHUNK, WIDTH), lambda i, d, c: (i, cidx(c, d), C_VA // WIDTH)),
            pl.BlockSpec((1, CHUNK, N_SMALL), lambda i, d, c: (i, cidx(c, d), 0)),
            pl.BlockSpec((1, N_SMALL), lambda i, d, c: (0, 0)),
        ],
        out_specs=pl.BlockSpec((1, 1, CHUNK, WIDTH), lambda i, d, c: (i, d, cidx(c, d), 0)),
        out_shape=jax.ShapeDtypeStruct((n, 2, tp, WIDTH), F32),
        scratch_shapes=[
            pltpu.VMEM((MLSTM_HEADS, MLSTM_DQK, MLSTM_DV), F32),
            pltpu.VMEM((MLSTM_HEADS, 1, MLSTM_DQK), F32),
            pltpu.VMEM((MLSTM_HEADS, 1, LANES), F32),
        ],
        compiler_params=_params(dimension_semantics=("parallel", "parallel", "arbitrary")),
        name="mlstm_scan",
    )(zm, zm, zm, zs, small_bias)


def _mlstm_post_body(h_ref, oa_ref, ga_ref, nw_ref, o_ref):
    h = h_ref[0, 0] + h_ref[0, 1]
    for j in range(MLSTM_HEADS):
        sl = slice(j * MLSTM_DV, (j + 1) * MLSTM_DV)
        hh = h[:, sl]
        hn = hh * lax.rsqrt(jnp.mean(hh * hh, axis=1, keepdims=True) + RMS_EPS) * nw_ref[:, sl]
        o_ref[0, :, sl] = (hn * _sigmoid(oa_ref[0][:, sl]) * _silu(ga_ref[0][:, sl])).astype(o_ref.dtype)


def _mlstm_post(hd, zm, norm_w):
    n, _, tp, _ = hd.shape
    tr = _tile(tp, 512, 16)
    return pl.pallas_call(
        _mlstm_post_body,
        grid=(n, tp // tr),
        in_specs=[
            pl.BlockSpec((1, 2, tr, WIDTH), lambda i, r: (i, 0, r, 0)),
            pl.BlockSpec((1, tr, WIDTH), lambda i, r: (i, r, C_OA // WIDTH)),
            pl.BlockSpec((1, tr, WIDTH), lambda i, r: (i, r, C_GA // WIDTH)),
            pl.BlockSpec((1, WIDTH), lambda i, r: (0, 0)),
        ],
        out_specs=pl.BlockSpec((1, tr, WIDTH), lambda i, r: (i, r, 0)),
        out_shape=jax.ShapeDtypeStruct((n, tp, WIDTH), BF16),
        compiler_params=_params(dimension_semantics=("parallel", "parallel")),
        name="mlstm_post",
    )(hd, zm, zm, norm_w.reshape(1, WIDTH))


def _cumsum_rows(tri_bf16, x):
    hi = x.astype(BF16)
    r1 = x - hi.astype(F32)
    mid = r1.astype(BF16)
    lo = (r1 - mid.astype(F32)).astype(BF16)
    dot = lambda t: jnp.dot(tri_bf16, t, preferred_element_type=F32)
    return dot(hi) + dot(mid) + dot(lo)


def _hgrn_body(q_ref, f_ref, v_ref, lb_ref, o_ref, st_sc, *, nc, lreal, rev):
    c = pl.program_id(1)
    cc = nc - 1 - c if rev else c

    @pl.when(c == 0)
    def _():
        st_sc[...] = jnp.zeros_like(st_sc)

    L = CHUNK
    sub = 8
    row = lax.broadcasted_iota(jnp.int32, (L, 1), 0)
    valid = (cc * L + row) < lreal
    r_i = lax.broadcasted_iota(jnp.int32, (L, L), 0)
    c_i = lax.broadcasted_iota(jnp.int32, (L, L), 1)
    tri = ((c_i >= r_i) if rev else (c_i <= r_i)).astype(BF16)
    ones = jnp.ones((HGRN_EXPAND, HGRN_EXPAND), BF16)

    def head(h, carry):
        sl = pl.ds(pl.multiple_of(h * HGRN_EXPAND, HGRN_EXPAND), HGRN_EXPAND)
        q = _silu(q_ref[0, :, sl])
        fpre = f_ref[0, :, sl]
        v = v_ref[0, :, sl]
        lb = lb_ref[0, :, sl]
        log_f = jnp.where(valid, jnp.log(lb + (1.0 - lb) * _sigmoid(fpre)), 0.0)
        kk = jnp.where(valid, (1.0 - lb) * _sigmoid(-fpre), 0.0)
        b = _cumsum_rows(tri, log_f)
        acc = [jnp.zeros((sub, HGRN_EXPAND), F32) for _ in range(L // sub)]
        for j0 in range(L // sub):
            lo_v, hi_v = (0, j0 + 1) if rev else (j0, L // sub)
            nr = (hi_v - lo_v) * sub
            rows = slice(lo_v * sub, hi_v * sub)
            pieces = []
            for s in range(j0 * sub, (j0 + 1) * sub):
                dlt = b[rows] - b[s:s + 1]
                seen = (row[rows] <= s) if rev else (row[rows] >= s)
                pieces.append((q[rows] * kk[s:s + 1]) * jnp.exp(jnp.where(seen, dlt, NEG)))
            red = jnp.dot(jnp.concatenate(pieces, axis=0).astype(BF16), ones, preferred_element_type=F32)
            for idx in range(sub):
                s = j0 * sub + idx
                for jj in range(hi_v - lo_v):
                    r0 = idx * nr + jj * sub
                    acc[lo_v + jj] = acc[lo_v + jj] + red[r0:r0 + sub] * v[s:s + 1]
        bl = b[0:1] if rev else b[L - 1:L]
        st = st_sc[h]
        o_ref[0, :, sl] = jnp.concatenate(acc, axis=0) + _dot_nt((q * jnp.exp(b)).astype(BF16), st.astype(BF16))
        st_sc[h] = st * jnp.exp(bl) + _dot_tn(v, kk * jnp.exp(bl - b))
        return carry

    lax.fori_loop(0, HGRN_HEADS, head, 0)


def _hgrn_scan(zm, lb, lreal, rev):
    n, tp, _ = zm.shape
    nc = tp // CHUNK
    cidx = (lambda c: nc - 1 - c) if rev else (lambda c: c)
    return pl.pallas_call(
        functools.partial(_hgrn_body, nc=nc, lreal=lreal, rev=rev),
        grid=(n, nc),
        in_specs=[
            pl.BlockSpec((1, CHUNK, WIDTH), lambda i, c: (i, cidx(c), C_QB // WIDTH)),
            pl.BlockSpec((1, CHUNK, WIDTH), lambda i, c: (i, cidx(c), C_FB // WIDTH + int(rev))),
            pl.BlockSpec((1, CHUNK, WIDTH), lambda i, c: (i, cidx(c), C_IB // WIDTH)),
            pl.BlockSpec((1, 1, WIDTH), lambda i, c: (int(rev), 0, 0)),
        ],
        out_specs=pl.BlockSpec((1, CHUNK, WIDTH), lambda i, c: (i, cidx(c), 0)),
        out_shape=jax.ShapeDtypeStruct((n, tp, WIDTH), F32),
        scratch_shapes=[pltpu.VMEM((HGRN_HEADS, HGRN_EXPAND, HGRN_EXPAND), F32)],
        compiler_params=_params(dimension_semantics=("parallel", "arbitrary")),
        name="hgrn_scan_bwd" if rev else "hgrn_scan_fwd",
    )(zm, zm, zm, lb)


def _hgrn_post_body(of_ref, ob_ref, gb_ref, nw_ref, y_ref):
    o = of_ref[0] + ob_ref[0]
    for j in range(HGRN_HEADS):
        sl = slice(j * HGRN_EXPAND, (j + 1) * HGRN_EXPAND)
        oh = o[:, sl]
        on = oh * lax.rsqrt(jnp.mean(oh * oh, axis=1, keepdims=True) + RMS_EPS) * nw_ref[...]
        y_ref[0, :, sl] = (on * _silu(gb_ref[0][:, sl])).astype(y_ref.dtype)


def _hgrn_post(of, ob, zm, norm_w):
    n, tp, _ = of.shape
    tr = _tile(tp, 512, 16)
    spec = pl.BlockSpec((1, tr, WIDTH), lambda i, r: (i, r, 0))
    return pl.pallas_call(
        _hgrn_post_body,
        grid=(n, tp // tr),
        in_specs=[spec, spec,
                  pl.BlockSpec((1, tr, WIDTH), lambda i, r: (i, r, C_GB // WIDTH)),
                  pl.BlockSpec((1, HGRN_EXPAND), lambda i, r: (0, 0))],
        out_specs=spec,
        out_shape=jax.ShapeDtypeStruct((n, tp, WIDTH), BF16),
        compiler_params=_params(dimension_semantics=("parallel", "parallel")),
        name="hgrn_post",
    )(of, ob, zm, norm_w.reshape(1, HGRN_EXPAND))


def _shift_body(x_ref, mu_ref, o_ref, *, lreal):
    x = x_ref[0]
    tp = x.shape[0]
    row = lax.broadcasted_iota(jnp.int32, (tp, 1), 0)
    prev = jnp.where(row == 0, 0.0, pltpu.roll(x, 1, 0))
    nxt = jnp.where(row + 1 >= lreal, 0.0, pltpu.roll(x, tp - 1, 0))
    o_ref[0] = x + mu_ref[0:1, :] * (prev - x) + mu_ref[1:2, :] * (nxt - x)


def _token_shift(z, col0, ncols, mu, lreal):
    n, tp, _ = z.shape
    tc = LANES
    return pl.pallas_call(
        functools.partial(_shift_body, lreal=lreal),
        grid=(n, ncols // tc),
        in_specs=[
            pl.BlockSpec((1, tp, tc), lambda i, j: (i, 0, col0 // tc + j)),
            pl.BlockSpec((2, tc), lambda i, j: (0, j)),
        ],
        out_specs=pl.BlockSpec((1, tp, tc), lambda i, j: (i, 0, j)),
        out_shape=jax.ShapeDtypeStruct((n, tp, ncols), F32),
        compiler_params=_params(dimension_semantics=("parallel", "parallel")),
        name="token_shift",
    )(z, mu)


def _seg_ones():
    r = lax.broadcasted_iota(jnp.int32, (LANES, LANES), 0) // RWKV_HEAD
    c = lax.broadcasted_iota(jnp.int32, (LANES, LANES), 1) // RWKV_HEAD
    return (r == c).astype(BF16)


def _head_sum(x, seg):
    hi = x.astype(BF16)
    r1 = x - hi.astype(F32)
    mid = r1.astype(BF16)
    lo = (r1 - mid.astype(F32)).astype(BF16)
    parts = []
    for j in range(WIDTH // LANES):
        sl = slice(j * LANES, (j + 1) * LANES)
        parts.append(jnp.dot(hi[:, sl], seg, preferred_element_type=F32)
                     + jnp.dot(mid[:, sl], seg, preferred_element_type=F32)
                     + jnp.dot(lo[:, sl], seg, preferred_element_type=F32))
    return jnp.concatenate(parts, axis=1)


def _rwkv_pre_body(f_ref, fs_ref, w2_ref, a2_ref, w0_ref, a0_ref, kk_ref, ka_ref, rk_ref,
                   r_o, v_o, kk_o, wf_o, wb_o, nf_o, nb_o, kf_o, kb_o, bonus_o, *, lreal):
    f = f_ref[0]
    r = f[:, 0:WIDTH]
    k = f[:, WIDTH:2 * WIDTH]
    v = f[:, 2 * WIDTH:3 * WIDTH]
    fs = fs_ref[0]
    wl = fs[:, 0:LANES]
    al = fs[:, LANES:2 * LANES]
    seg = _seg_ones()
    w = w0_ref[...] + jnp.dot(jnp.tanh(wl).astype(BF16), w2_ref[...], preferred_element_type=F32)
    wdec = jnp.exp(-jnp.exp(-_softplus(-w) - 0.5))
    a = _sigmoid(a0_ref[...] + jnp.dot(al.astype(BF16), a2_ref[...], preferred_element_type=F32))
    kk = k * kk_ref[...]
    kk = kk / jnp.maximum(jnp.sqrt(_head_sum(kk * kk, seg)), 1e-12)
    a_f, a_b = a[:, 0:WIDTH], a[:, WIDTH:2 * WIDTH]
    kd_f = k * (1.0 + (a_f - 1.0) * ka_ref[...])
    kd_b = k * (1.0 + (a_b - 1.0) * ka_ref[...])
    tr = f.shape[0]
    valid = (pl.program_id(1) * tr + lax.broadcasted_iota(jnp.int32, (tr, 1), 0)) < lreal
    keep = lambda x: jnp.where(valid, x, 0.0)

    def put(o_ref, x):
        for p in range(WIDTH // LANES):
            o_ref[0, :, p, :] = x[:, p * LANES:(p + 1) * LANES]

    put(r_o, keep(r))
    put(v_o, keep(v))
    put(kk_o, keep(kk))
    put(wf_o, jnp.where(valid, wdec[:, 0:WIDTH], 1.0))
    put(wb_o, jnp.where(valid, wdec[:, WIDTH:2 * WIDTH], 1.0))
    put(nf_o, keep(-(kk * a_f)))
    put(nb_o, keep(-(kk * a_b)))
    put(kf_o, keep(kd_f))
    put(kb_o, keep(kd_b))
    bonus_o[0] = _head_sum(r * (kd_f + kd_b) * rk_ref[...], seg) * v


def _rwkv_pre(fsh, fsm, w2cat, a2cat, w0, a0, k_k, k_a, r_k, lreal):
    n, tp, _ = fsh.shape
    tr = _tile(tp, 256, 8)
    row = lambda width: pl.BlockSpec((1, width), lambda i, r: (0, 0))
    out_spec = pl.BlockSpec((1, tr, WIDTH), lambda i, r: (i, r, 0))
    return pl.pallas_call(
        functools.partial(_rwkv_pre_body, lreal=lreal),
        grid=(n, tp // tr),
        in_specs=[
            pl.BlockSpec((1, tr, 3 * WIDTH), lambda i, r: (i, r, 0)),
            pl.BlockSpec((1, tr, 2 * LANES), lambda i, r: (i, r, 0)),
            pl.BlockSpec((LANES, 2 * WIDTH), lambda i, r: (0, 0)),
            pl.BlockSpec((LANES, 2 * WIDTH), lambda i, r: (0, 0)),
            row(2 * WIDTH), row(2 * WIDTH), row(WIDTH), row(WIDTH), row(WIDTH),
        ],
        out_specs=[pl.BlockSpec((1, tr, WIDTH // LANES, LANES), lambda i, r: (i, r, 0, 0))] * 9 + [out_spec],
        out_shape=[jax.ShapeDtypeStruct((n, tp, WIDTH // LANES, LANES), F32)] * 9
        + [jax.ShapeDtypeStruct((n, tp, WIDTH), F32)],
        compiler_params=_params(dimension_semantics=("parallel", "parallel")),
        name="rwkv_pre",
    )(fsh, fsm, w2cat, a2cat, w0, a0, k_k, k_a, r_k)


RWKV_PAIRS = WIDTH // LANES
RWKV_TB = 16
RWKV_NV1 = 8


def _sum_interleaved(terms, nacc):
    accs = []
    for i, t in enumerate(terms):
        if i < nacc:
            accs.append(t)
        else:
            accs[i % nacc] = accs[i % nacc] + t
    while len(accs) > 1:
        accs = [accs[i] + accs[i + 1] for i in range(0, len(accs), 2)]
    return accs[0]


def _rwkv_chain_step(s_sc, t_sc, load_v, g, nacc):
    base = g * RWKV_HEAD
    row = lambda i, k: t_sc[i, base + k:base + k + 1, :]
    sa = _sum_interleaved((s_sc[g, k] * row(1, k) for k in range(RWKV_HEAD)), nacc)

    def update(k):
        s = s_sc[g, k] * row(0, k) + sa * row(2, k) + load_v() * row(3, k)
        s_sc[g, k] = s
        return s * row(4, k)

    return _sum_interleaved((update(k) for k in range(RWKV_HEAD)), nacc)


def _rwkv_pipelined_steps(build, compute, t_a, t_b):
    build(0, t_a)

    def pair(j2, carry):
        j = 2 * j2
        build(j + 1, t_b)
        compute(j, t_a)
        build(jnp.minimum(j + 2, RWKV_TB - 1), t_a)
        compute(j + 1, t_b)
        return carry

    lax.fori_loop(0, RWKV_TB // 2, pair, 0)


def _rwkv_scan8_body(wf, wb, kkf, kkb, nf, nb, kdf, kdb, rf, rb, vf, vb, yf_ref, yb_ref, s_sc, t_a, t_b):
    @pl.when(pl.program_id(0) == 0)
    def _():
        s_sc[...] = jnp.zeros_like(s_sc)

    nseq = wf.shape[0]
    operands = ((wf, wb), (kkf, kkb), (nf, nb), (kdf, kdb), (rf, rb), (vf, vb))

    def build(j, t_sc):
        jb = RWKV_TB - 1 - j
        for i, (xf_ref, xb_ref) in enumerate(operands):
            rows = [xf_ref[s, j] for s in range(nseq)] + [xb_ref[s, jb] for s in range(nseq)]
            t_sc[i] = jnp.concatenate(rows, axis=0).T

    def compute(j, t_sc):
        jb = RWKV_TB - 1 - j
        ys = [_rwkv_chain_step(s_sc, t_sc, lambda g=g: t_sc[5, g * RWKV_HEAD:(g + 1) * RWKV_HEAD, :], g, 4)
              for g in range(2)]
        yt = jnp.concatenate(ys, axis=0).T
        for s in range(nseq):
            yf_ref[s, j] = yt[s * RWKV_PAIRS:(s + 1) * RWKV_PAIRS]
            yb_ref[s, jb] = yt[(nseq + s) * RWKV_PAIRS:(nseq + s + 1) * RWKV_PAIRS]

    _rwkv_pipelined_steps(build, compute, t_a, t_b)


def _rwkv_scan1_body(wf, wb, kkf, kkb, nf, nb, kdf, kdb, rf, rb, vf, vb, yf_ref, yb_ref, s_sc, t_a, t_b):
    @pl.when(pl.program_id(0) == 0)
    def _():
        s_sc[...] = jnp.zeros_like(s_sc)

    nv = RWKV_NV1
    noct = RWKV_HEAD // nv
    operands = ((wf, wb), (kkf, kkb), (nf, nb), (kdf, kdb), (rf, rb), (vf, vb))

    def build(j, t_sc):
        jb = RWKV_TB - 1 - j
        for i, (xf_ref, xb_ref) in enumerate(operands):
            rows = [jnp.broadcast_to(xf_ref[0, j, p:p + 1, :], (8, LANES)) for p in range(RWKV_PAIRS)]
            rows += [jnp.broadcast_to(xb_ref[0, jb, p:p + 1, :], (8, LANES)) for p in range(RWKV_PAIRS)]
            t_sc[i] = jnp.concatenate(rows, axis=0).T

    def compute(j, t_sc):
        jb = RWKV_TB - 1 - j
        octet = lax.broadcasted_iota(jnp.int32, (nv, LANES), 1) % noct
        ys = []
        for g in range(2):
            v_g = jnp.zeros((nv, LANES), F32)
            for q in range(noct):
                r0 = g * RWKV_HEAD + q * nv
                v_g = jnp.where(octet == q, t_sc[5, r0:r0 + nv, :], v_g)
            ys.append(_rwkv_chain_step(s_sc, t_sc, lambda v_g=v_g: v_g, g, 8))
        zrows = [jnp.where(octet == q, ys[g], 0.0) for g in range(2) for q in range(noct)]
        yt = jnp.concatenate(zrows, axis=0).T
        for dp in range(2 * RWKV_PAIRS):
            rowv = jnp.sum(yt[dp * 8:(dp + 1) * 8], axis=0, keepdims=True)
            p = dp % RWKV_PAIRS
            if dp < RWKV_PAIRS:
                yf_ref[0, j, p:p + 1, :] = rowv
            else:
                yb_ref[0, jb, p:p + 1, :] = rowv

    _rwkv_pipelined_steps(build, compute, t_a, t_b)


def _rwkv_scan(wf, wb, kk, nf, nb, kdf, kdb, r, v):
    n, tp = r.shape[:2]
    nblk = tp // RWKV_TB
    fspec = pl.BlockSpec((n, RWKV_TB, RWKV_PAIRS, LANES), lambda i: (0, i, 0, 0))
    bspec = pl.BlockSpec((n, RWKV_TB, RWKV_PAIRS, LANES), lambda i: (0, nblk - 1 - i, 0, 0))
    if n == 8:
        body, nv = _rwkv_scan8_body, RWKV_HEAD
    else:
        assert n == 1
        body, nv = _rwkv_scan1_body, RWKV_NV1
    return pl.pallas_call(
        body,
        grid=(nblk,),
        in_specs=[fspec, bspec] * 6,
        out_specs=[fspec, bspec],
        out_shape=[jax.ShapeDtypeStruct((n, tp, RWKV_PAIRS, LANES), F32)] * 2,
        scratch_shapes=[pltpu.VMEM((2, RWKV_HEAD, nv, LANES), F32),
                        pltpu.VMEM((6, LANES, LANES), F32), pltpu.VMEM((6, LANES, LANES), F32)],
        compiler_params=_params(dimension_semantics=("arbitrary",)),
        name="rwkv_scan",
    )(wf, wb, kk, kk, nf, nb, kdf, kdb, r, r, v, v)


def _rwkv_post_body(yf_ref, yb_ref, bonus_ref, gc_ref, lnw_ref, lnb_ref, o_ref):
    seg = _seg_ones()
    y = jnp.concatenate([yf_ref[0, :, p, :] + yb_ref[0, :, p, :] for p in range(RWKV_PAIRS)], axis=1)
    mean = _head_sum(y, seg) * (1.0 / RWKV_HEAD)
    yc = y - mean
    var = _head_sum(yc * yc, seg) * (1.0 / RWKV_HEAD)
    yn = yc * lax.rsqrt(var + RWKV_GN_EPS) * lnw_ref[...] + lnb_ref[...]
    o_ref[0] = ((yn + bonus_ref[0]) * _silu(gc_ref[0])).astype(o_ref.dtype)


def _rwkv_post(yf, yb, bonus, zm, ln_w, ln_b):
    n, tp = yf.shape[:2]
    tr = _tile(tp, 512, 16)
    spec = pl.BlockSpec((1, tr, WIDTH), lambda i, r: (i, r, 0))
    yspec = pl.BlockSpec((1, tr, RWKV_PAIRS, LANES), lambda i, r: (i, r, 0, 0))
    row = pl.BlockSpec((1, WIDTH), lambda i, r: (0, 0))
    return pl.pallas_call(
        _rwkv_post_body,
        grid=(n, tp // tr),
        in_specs=[yspec, yspec, spec, pl.BlockSpec((1, tr, WIDTH), lambda i, r: (i, r, C_GC // WIDTH)), row, row],
        out_specs=spec,
        out_shape=jax.ShapeDtypeStruct((n, tp, WIDTH), BF16),
        compiler_params=_params(dimension_semantics=("parallel", "parallel")),
        name="rwkv_post",
    )(yf, yb, bonus, zm, ln_w.reshape(1, WIDTH), ln_b.reshape(1, WIDTH))


def _rwkv_branch(zm, zs, p, lreal):
    fsh = _token_shift(zm, C_RC, 3 * WIDTH, p['mu_main'], lreal)
    fsm = _token_shift(zs, 0, 2 * LANES, p['mu_small'], lreal)
    r, v, kk, wf, wb, nf, nb, kf, kb, bonus = _rwkv_pre(
        fsh, fsm, p['w2cat'], p['a2cat'], p['w0'], p['a0'], p['k_k'], p['k_a'], p['r_k'], lreal)
    yf, yb = _rwkv_scan(wf, wb, kk, nf, nb, kf, kb, r, v)
    return _rwkv_post(yf, yb, bonus, zm, p['ln_w'], p['ln_b'])


def _s5_core_body(u_ref, m_ref, n_ref, q_ref, pw_ref, cidx_ref, y_ref, *, nsteps, nvalid, nblocks):
    cidx = cidx_ref[...]
    u = jnp.where(cidx < nvalid, u_ref[0], 0.0).astype(BF16)
    rows = u.shape[0]
    y = None
    for d in range(2):
        yd = jnp.dot(u, m_ref[d, 0], preferred_element_type=F32)
        x = jnp.dot(u, n_ref[d, 0], preferred_element_type=F32)
        for j in range(nsteps):
            s = 1 << j
            if d == 0:
                xs = jnp.where(cidx >= s, pltpu.roll(x, s, 0), 0.0)
            else:
                xs = jnp.where(cidx + s < nblocks, pltpu.roll(x, rows - s, 0), 0.0)
            x = (x + xs * pw_ref[d, 0, 2 * j:2 * j + 1, :]
                 + pltpu.roll(xs, S5_STATE, 1) * pw_ref[d, 0, 2 * j + 1:2 * j + 2, :])
        if d == 0:
            xin = jnp.where(cidx >= 1, pltpu.roll(x, 1, 0), 0.0)
        else:
            xin = jnp.where(cidx + 1 < nblocks, pltpu.roll(x, rows - 1, 0), 0.0)
        yd = yd + jnp.dot(xin.astype(BF16), q_ref[d, 0], preferred_element_type=F32)
        y = yd if y is None else y + yd
    y_ref[0] = y


def _s5_core(u, mats, cidx, nsteps, nvalid, nblocks):
    g, rows, _ = u.shape
    mm, nn, qq, pw = mats
    return pl.pallas_call(
        functools.partial(_s5_core_body, nsteps=nsteps, nvalid=nvalid, nblocks=nblocks),
        grid=(g,),
        in_specs=[
            pl.BlockSpec((1, rows, S5_ROW), lambda j: (j, 0, 0)),
            pl.BlockSpec((2, 1, S5_ROW, S5_ROW), lambda j: (0, j, 0, 0)),
            pl.BlockSpec((2, 1, S5_ROW, 2 * S5_STATE), lambda j: (0, j, 0, 0)),
            pl.BlockSpec((2, 1, 2 * S5_STATE, S5_ROW), lambda j: (0, j, 0, 0)),
            pl.BlockSpec((2, 1, pw.shape[2], 2 * S5_STATE), lambda j: (0, j, 0, 0)),
            pl.BlockSpec((rows, 1), lambda j: (0, 0)),
        ],
        out_specs=pl.BlockSpec((1, rows, S5_ROW), lambda j: (j, 0, 0)),
        out_shape=jax.ShapeDtypeStruct((g, rows, S5_ROW), F32),
        compiler_params=_params(dimension_semantics=("parallel",)),
        name="s5_core",
    )(u, mm, nn, qq, pw, cidx)


def _s5_post_body(y_ref, ud_ref, gd_ref, d_ref, w_ref, b_ref, o_ref):
    y = y_ref[0] + d_ref[...] * ud_ref[0]
    g = 0.5 * y * (1.0 + jnp.tanh(0.7978845608028654 * (y + 0.044715 * (y * y * y))))
    glu = jnp.dot(g.astype(BF16), w_ref[...], preferred_element_type=F32) + b_ref[...]
    o_ref[0] = (g * _sigmoid(glu) * _silu(gd_ref[0])).astype(o_ref.dtype)


def _s5_post(ys, zm, d, glu_w, glu_b):
    n, tp, _ = ys.shape
    tr = _tile(tp, 512, 16)
    spec = pl.BlockSpec((1, tr, WIDTH), lambda i, r: (i, r, 0))
    row = pl.BlockSpec((1, WIDTH), lambda i, r: (0, 0))
    return pl.pallas_call(
        _s5_post_body,
        grid=(n, tp // tr),
        in_specs=[spec,
                  pl.BlockSpec((1, tr, WIDTH), lambda i, r: (i, r, C_UD // WIDTH)),
                  pl.BlockSpec((1, tr, WIDTH), lambda i, r: (i, r, C_GD // WIDTH)),
                  row, pl.BlockSpec((WIDTH, WIDTH), lambda i, r: (0, 0)), row],
        out_specs=spec,
        out_shape=jax.ShapeDtypeStruct((n, tp, WIDTH), BF16),
        compiler_params=_params(dimension_semantics=("parallel", "parallel")),
        name="s5_post",
    )(ys, zm, zm, d.reshape(1, WIDTH), glu_w, glu_b.reshape(1, WIDTH))


def _s5_mats(a_re, a_im, log_dt, b_re, b_im, c_re, c_im, max_blocks):
    dt = jnp.exp(log_dt)[..., None]
    mag = jnp.exp(a_re * dt)
    ang = a_im * dt
    ab_re, ab_im = mag * jnp.cos(ang), mag * jnp.sin(ang)
    den = a_re * a_re + a_im * a_im
    xr, yi = ab_re - 1.0, ab_im
    coef_re = (xr * a_re + yi * a_im) / den
    coef_im = (yi * a_re - xr * a_im) / den
    bb_re = coef_re[..., None] * b_re - coef_im[..., None] * b_im
    bb_im = coef_re[..., None] * b_im + coef_im[..., None] * b_re
    pr, pi = [jnp.ones_like(ab_re)], [jnp.zeros_like(ab_re)]
    for _ in range(S5_BLOCK):
        pr, pi = pr + [pr[-1] * ab_re - pi[-1] * ab_im], pi + [pr[-1] * ab_im + pi[-1] * ab_re]
    pw_re, pw_im = jnp.stack(pr), jnp.stack(pi)
    t_re = pw_re[..., None] * bb_re - pw_im[..., None] * bb_im
    t_im = pw_re[..., None] * bb_im + pw_im[..., None] * bb_re
    kj = (jnp.einsum('dgcp,jdgpe->jdgce', c_re, t_re, precision=HI)
          - jnp.einsum('dgcp,jdgpe->jdgce', c_im, t_im, precision=HI))
    s_i = jnp.arange(S5_BLOCK)[:, None]
    i_i = jnp.arange(S5_BLOCK)[None, :]
    rev = jnp.arange(S5_BLOCK - 1, -1, -1)

    def tables(d):
        dist = (s_i - i_i) if d else (i_i - s_i)
        m6 = jnp.where((dist >= 0)[:, :, None, None, None], kj[jnp.clip(dist, 0, S5_BLOCK), d], 0.0)
        mm_d = jnp.transpose(m6, (2, 0, 4, 1, 3)).reshape(S5_GROUPS, S5_ROW, S5_ROW)
        order = jnp.arange(S5_BLOCK) if d else rev
        n_re = jnp.transpose(t_re[order, d], (1, 0, 3, 2)).reshape(S5_GROUPS, S5_ROW, S5_STATE)
        n_im = jnp.transpose(t_im[order, d], (1, 0, 3, 2)).reshape(S5_GROUPS, S5_ROW, S5_STATE)
        steps_in = (rev + 1) if d else (jnp.arange(S5_BLOCK) + 1)
        pr_i, pi_i = pw_re[steps_in, d][:, :, None, :], pw_im[steps_in, d][:, :, None, :]
        ca_re = c_re[d][None] * pr_i - c_im[d][None] * pi_i
        ca_im = c_re[d][None] * pi_i + c_im[d][None] * pr_i
        q_re = jnp.transpose(ca_re, (1, 3, 0, 2)).reshape(S5_GROUPS, S5_STATE, S5_ROW)
        q_im = jnp.transpose(-ca_im, (1, 3, 0, 2)).reshape(S5_GROUPS, S5_STATE, S5_ROW)
        return mm_d, jnp.concatenate([n_re, n_im], axis=-1), jnp.concatenate([q_re, q_im], axis=1)

    mm, nn, qq = (jnp.stack(t).astype(BF16) for t in zip(tables(0), tables(1)))
    sr, si = pw_re[S5_BLOCK], pw_im[S5_BLOCK]
    rows = []
    steps = 0
    while (1 << steps) < max_blocks:
        rows += [jnp.concatenate([sr, sr], axis=-1), jnp.concatenate([-si, si], axis=-1)]
        sr, si = sr * sr - si * si, 2.0 * sr * si
        steps += 1
    while len(rows) % 8:
        rows.append(jnp.zeros_like(rows[0]))
    pw = jnp.stack(rows, axis=2)
    return (mm, nn, qq, pw), steps


def _s5_branch(zm, p, lreal):
    n, tp, _ = zm.shape
    assert lreal % S5_BLOCK == 0 and tp % S5_BLOCK == 0
    nvalid = lreal // S5_BLOCK
    nb = tp // S5_BLOCK
    rows = n * nb
    rows_p = -(-rows // 8) * 8
    u = zm[:, :, C_UD:C_UD + WIDTH].reshape(n, nb, S5_BLOCK, S5_GROUPS, S5_GROUP)
    u = jnp.transpose(u, (3, 0, 1, 2, 4)).reshape(S5_GROUPS, rows, S5_ROW)
    u = jnp.pad(u, ((0, 0), (0, rows_p - rows), (0, 0)))
    cidx = jnp.pad(jnp.tile(jnp.arange(nb, dtype=jnp.int32), n), (0, rows_p - rows),
                   constant_values=nb).reshape(rows_p, 1)
    mats, nsteps = p['s5_mats'](nb)
    y = _s5_core(u, mats, cidx, nsteps, nvalid, nb)[:, :rows]
    ys = jnp.transpose(y.reshape(S5_GROUPS, n, nb, S5_BLOCK, S5_GROUP), (1, 2, 3, 0, 4)).reshape(n, tp, WIDTH)
    return _s5_post(ys, zm, p['s5_d'], p['glu_w'], p['glu_b'])


def _merge_body(h_ref, ya_ref, yb_ref, yc_ref, yd_ref, wg_ref, bg_ref, wb_ref, o_ref):
    h = h_ref[...]
    acc = None
    for b, y_ref in enumerate((ya_ref, yb_ref, yc_ref, yd_ref)):
        gate = _sigmoid(jnp.dot(h, wg_ref[b], preferred_element_type=F32) + bg_ref[b])
        term = gate * jnp.dot(y_ref[...], wb_ref[b], preferred_element_type=F32)
        acc = term if acc is None else acc + term
    o_ref[...] = acc.astype(o_ref.dtype)


def _merge(h, ys, wg, bg, wb):
    m, d = h.shape
    tm = _tile(m, 1024, 16)
    tn = 256
    yspec = pl.BlockSpec((tm, WIDTH), lambda i, j: (i, 0))
    return pl.pallas_call(
        _merge_body,
        grid=(m // tm, d // tn),
        in_specs=[pl.BlockSpec((tm, d), lambda i, j: (i, 0)), yspec, yspec, yspec, yspec,
                  pl.BlockSpec((4, d, tn), lambda i, j: (0, 0, j)),
                  pl.BlockSpec((4, 1, tn), lambda i, j: (0, 0, j)),
                  pl.BlockSpec((4, WIDTH, tn), lambda i, j: (0, 0, j))],
        out_specs=pl.BlockSpec((tm, tn), lambda i, j: (i, j)),
        out_shape=jax.ShapeDtypeStruct((m, d), BF16),
        compiler_params=_params(dimension_semantics=("parallel", "parallel")),
        name="merge",
    )(h, *ys, wg, bg, wb)


def _layer(xp, p, lreal):
    n, tp, d = xp.shape
    m = n * tp
    x2 = xp.reshape(m, d)
    h = _rmsnorm(x2, p['norm_w'], BF16)
    zm = _mm(h, p['w_main'], name="proj_main").reshape(n, tp, N_MAIN)
    zs = _mm(h, p['w_small'], name="proj_small").reshape(n, tp, N_SMALL)
    ya = _mlstm_post(_mlstm_scan(zm, zs, p['small_bias'], lreal), zm, p['mlstm_norm_w'])
    yb = _hgrn_post(_hgrn_scan(zm, p['hgrn_lb'], lreal, False), _hgrn_scan(zm, p['hgrn_lb'], lreal, True), zm,
                    p['hgrn_norm_w'])
    yc = _rwkv_branch(zm, zs, p, lreal)
    yd = _s5_branch(zm, p, lreal)
    ys = [y.reshape(m, WIDTH) for y in (ya, yb, yc, yd)]
    merged = _merge(h, ys, p['w_gate'], p['b_gate'], p['w_branch'])
    return _mm(merged, p['w_out'], res=x2, name="proj_out").reshape(n, tp, d)


def _trunk(x, meta, layers, final_norm_w):
    n, t, d = x.shape
    lreal = t + N_META
    tp = lreal + TAIL_PAD
    xp = jnp.concatenate([jnp.broadcast_to(meta[None], (n, N_META, d)), x, jnp.zeros((n, TAIL_PAD, d), x.dtype)], axis=1)
    for p in layers:
        xp = _layer(xp, p, lreal)
    y = _rmsnorm(xp.reshape(n * tp, d), final_norm_w, F32).reshape(n, tp, d)
    return y[:, N_META:lreal]


def _layer_params(l, a):
    w_in = a['w_in'][l]
    offs = {}
    acc = 0
    sizes = (512, 512, WIDTH, WIDTH, 8, 8, WIDTH, WIDTH, 2 * WIDTH, WIDTH, WIDTH, WIDTH, WIDTH, WIDTH,
             2 * RWKV_LORA, 2 * RWKV_LORA, WIDTH, WIDTH, WIDTH)
    names = ('qa', 'ka', 'va', 'oa', 'iga', 'fga', 'ga', 'qb', 'fb', 'ib', 'gb', 'rc', 'kc', 'vc', 'wlc', 'alc',
             'gc', 'ud', 'gd')
    for nm, sz in zip(names, sizes):
        offs[nm] = (acc, acc + sz)
        acc += sz
    col = lambda nm: w_in[:, offs[nm][0]:offs[nm][1]]
    main_order = ('qa', 'ka', 'va', 'oa', 'ga', 'qb', 'fb', 'ib', 'gb', 'rc', 'kc', 'vc', 'gc', 'ud', 'gd')
    w_main = jnp.concatenate([col(nm) for nm in main_order], axis=1).astype(BF16)
    w_small = jnp.concatenate([col('wlc'), col('alc'), col('iga'), col('fga'),
                               jnp.zeros((D_MODEL, LANES - 16), F32)], axis=1).astype(BF16)
    small_bias = jnp.concatenate([jnp.zeros((G_OFF,), F32), a['mlstm_ig_b'][l].reshape(-1),
                                  a['mlstm_fg_b'][l].reshape(-1), jnp.zeros((LANES - 16,), F32)]).reshape(1, N_SMALL)
    lbw = jax.nn.softmax(a['hgrn_lower_bounds'].astype(F32), axis=1)
    hgrn_lb = (jnp.cumsum(lbw, axis=1) - lbw[:, :1])[:, l].reshape(2, 1, WIDTH)
    mu = a['rwkv_shift_mu'][l]
    zero = jnp.zeros((RWKV_LORA, WIDTH), F32)
    blockdiag = lambda w2: jnp.concatenate([jnp.concatenate([w2[0], zero], axis=1),
                                            jnp.concatenate([zero, w2[1]], axis=1)], axis=0)
    s5_args = tuple(a[k][l].astype(F32) for k in ('s5_a_re', 's5_a_im', 's5_log_dt', 's5_b_re', 's5_b_im',
                                                  's5_c_re', 's5_c_im'))
    return {
        'norm_w': a['norm_w'][l], 'w_main': w_main, 'w_small': w_small, 'small_bias': small_bias,
        'mlstm_norm_w': a['mlstm_norm_w'][l], 'hgrn_lb': hgrn_lb, 'hgrn_norm_w': a['hgrn_norm_w'][l],
        'mu_main': mu[:, :3 * WIDTH], 'mu_small': mu[:, 3 * WIDTH:],
        'w2cat': blockdiag(a['rwkv_w2'][l]).astype(BF16), 'a2cat': blockdiag(a['rwkv_a2'][l]).astype(BF16),
        'w0': a['rwkv_w0'][l].reshape(1, 2 * WIDTH), 'a0': a['rwkv_a0'][l].reshape(1, 2 * WIDTH),
        'k_k': a['rwkv_k_k'][l].reshape(1, WIDTH), 'k_a': a['rwkv_k_a'][l].reshape(1, WIDTH),
        'r_k': a['rwkv_r_k'][l].reshape(1, WIDTH),
        'ln_w': a['rwkv_ln_w'][l], 'ln_b': a['rwkv_ln_b'][l],
        's5_mats': functools.partial(_s5_mats, *s5_args),
        's5_d': a['s5_d'][l], 'glu_w': a['s5_glu_w'][l].astype(BF16), 'glu_b': a['s5_glu_b'][l],
        'w_gate': a['w_gate'][l].astype(BF16), 'b_gate': a['b_gate'][l].reshape(4, 1, D_MODEL),
        'w_branch': a['w_branch'][l].astype(BF16), 'w_out': a['w_out'][l].astype(BF16),
    }


def kernel(x_prompt, x_sample, meta_tokens, norm_w, w_in, mlstm_ig_b, mlstm_fg_b, mlstm_norm_w, hgrn_lower_bounds, hgrn_norm_w, rwkv_shift_mu, rwkv_w0, rwkv_w2, rwkv_a0, rwkv_a2, rwkv_k_k, rwkv_k_a, rwkv_r_k, rwkv_ln_w, rwkv_ln_b, s5_a_re, s5_a_im, s5_log_dt, s5_b_re, s5_b_im, s5_c_re, s5_c_im, s5_d, s5_glu_w, s5_glu_b, w_branch, w_gate, b_gate, w_out, final_norm_w):
    a = dict(norm_w=norm_w, w_in=w_in, mlstm_ig_b=mlstm_ig_b, mlstm_fg_b=mlstm_fg_b, mlstm_norm_w=mlstm_norm_w,
             hgrn_lower_bounds=hgrn_lower_bounds, hgrn_norm_w=hgrn_norm_w, rwkv_shift_mu=rwkv_shift_mu,
             rwkv_w0=rwkv_w0, rwkv_w2=rwkv_w2, rwkv_a0=rwkv_a0, rwkv_a2=rwkv_a2, rwkv_k_k=rwkv_k_k,
             rwkv_k_a=rwkv_k_a, rwkv_r_k=rwkv_r_k, rwkv_ln_w=rwkv_ln_w, rwkv_ln_b=rwkv_ln_b, s5_a_re=s5_a_re,
             s5_a_im=s5_a_im, s5_log_dt=s5_log_dt, s5_b_re=s5_b_re, s5_b_im=s5_b_im, s5_c_re=s5_c_re,
             s5_c_im=s5_c_im, s5_d=s5_d, s5_glu_w=s5_glu_w, s5_glu_b=s5_glu_b, w_branch=w_branch, w_gate=w_gate,
             b_gate=b_gate, w_out=w_out)
    layers = [_layer_params(l, a) for l in range(DEPTH)]
    meta = meta_tokens.astype(x_prompt.dtype)
    return (_trunk(x_prompt, meta, layers, final_norm_w), _trunk(x_sample, meta, layers, final_norm_w))
```

```python
import functools

import jax
import jax.numpy as jnp
from jax import lax
from jax.experimental import pallas as pl
from jax.experimental.pallas import tpu as pltpu

F32 = jnp.float32
BF16 = jnp.bfloat16
HI = lax.Precision.HIGHEST

D_MODEL = 2048
DEPTH = 2
N_META = 16
CHUNK = 64
TAIL_PAD = CHUNK - N_META
WIDTH = D_MODEL // 2
RMS_EPS = 1e-6
NEG = -1e30

MLSTM_HEADS = 4
MLSTM_DV = WIDTH // MLSTM_HEADS
MLSTM_DQK = MLSTM_DV // 2
HGRN_EXPAND = 128
HGRN_HEADS = WIDTH // HGRN_EXPAND
HGRN_SUB = 16
HGRN_GROUP = 8
RWKV_HEAD = 64
RWKV_HEADS = WIDTH // RWKV_HEAD
RWKV_LORA = 64
RWKV_GN_EPS = 64e-5
S5_GROUP = 16
S5_GROUPS = WIDTH // S5_GROUP
S5_STATE = 64
S5_BLOCK = 16
S5_ROW = S5_BLOCK * S5_GROUP

LANES = 128
VMEM_LIMIT = 48 * 1024 * 1024

C_QA, C_KA, C_VA, C_OA, C_GA = 0, 512, 1024, 2048, 3072
C_QB, C_FB, C_IB, C_GB = 4096, 5120, 7168, 8192
C_RC, C_KC, C_VC, C_GC = 9216, 10240, 11264, 12288
C_UD, C_GD = 13312, 14336
N_MAIN = 15360
N_SMALL = 384
G_OFF = 256


def _params(**kw):
    return pltpu.CompilerParams(vmem_limit_bytes=VMEM_LIMIT, **kw)


def _tile(total, cap, mult):
    best = None
    for t in range(mult, min(total, cap) + 1, mult):
        if total % t == 0:
            best = t
    assert best is not None, (total, cap, mult)
    return best


def _sigmoid(x):
    return jax.nn.sigmoid(x)


def _silu(x):
    return x * jax.nn.sigmoid(x)


def _softplus(x):
    return jnp.maximum(x, 0.0) + jnp.log(1.0 + jnp.exp(-jnp.abs(x)))


def _dot_nt(a, b, precision=None):
    return lax.dot_general(a, b, (((1,), (1,)), ((), ())), precision=precision, preferred_element_type=F32)


def _dot_tn(a, b, precision=None):
    return lax.dot_general(a, b, (((0,), (0,)), ((), ())), precision=precision, preferred_element_type=F32)


def _rmsnorm_body(x_ref, w_ref, o_ref):
    x = x_ref[...]
    y = x * lax.rsqrt(jnp.mean(x * x, axis=-1, keepdims=True) + RMS_EPS) * w_ref[...]
    o_ref[...] = y.astype(o_ref.dtype)


def _rmsnorm(x2d, w, out_dtype):
    m, d = x2d.shape
    tm = _tile(m, 512, 16)
    return pl.pallas_call(
        _rmsnorm_body,
        grid=(m // tm,),
        in_specs=[pl.BlockSpec((tm, d), lambda i: (i, 0)), pl.BlockSpec((1, d), lambda i: (0, 0))],
        out_specs=pl.BlockSpec((tm, d), lambda i: (i, 0)),
        out_shape=jax.ShapeDtypeStruct((m, d), out_dtype),
        compiler_params=_params(dimension_semantics=("parallel",)),
        name="rmsnorm",
    )(x2d, w.reshape(1, d).astype(F32))


def _mm_body(*refs, has_res):
    x_ref, w_ref = refs[0], refs[1]
    o_ref = refs[-1]
    acc = jnp.dot(x_ref[...], w_ref[...], preferred_element_type=F32)
    if has_res:
        acc = refs[2][...] + acc
    o_ref[...] = acc.astype(o_ref.dtype)


def _mm(x, w, res=None, out_dtype=F32, name="mm"):
    m, k = x.shape
    n = w.shape[1]
    tm = _tile(m, 1408, 16)
    tn = 1024 if n % 1024 == 0 else n
    in_specs = [pl.BlockSpec((tm, k), lambda i, j: (i, 0)), pl.BlockSpec((k, tn), lambda i, j: (0, j))]
    args = [x, w]
    if res is not None:
        in_specs.append(pl.BlockSpec((tm, tn), lambda i, j: (i, j)))
        args.append(res)
    return pl.pallas_call(
        functools.partial(_mm_body, has_res=res is not None),
        grid=(m // tm, n // tn),
        in_specs=in_specs,
        out_specs=pl.BlockSpec((tm, tn), lambda i, j: (i, j)),
        out_shape=jax.ShapeDtypeStruct((m, n), out_dtype),
        compiler_params=_params(dimension_semantics=("parallel", "parallel")),
        name=name,
    )(*args)


def _mlstm_body(q_ref, k_ref, v_ref, zs_ref, bias_ref, o_ref, c_sc, n_sc, m_sc, *, nc, lreal):
    d = pl.program_id(1)
    c = pl.program_id(2)
    cc = c + d * (nc - 1 - 2 * c)
    rev = d == 1

    @pl.when(c == 0)
    def _():
        c_sc[...] = jnp.zeros_like(c_sc)
        n_sc[...] = jnp.zeros_like(n_sc)
        m_sc[...] = jnp.zeros_like(m_sc)

    L = CHUNK
    row = lax.broadcasted_iota(jnp.int32, (L, 1), 0)
    valid = (cc * L + row) < lreal
    lane = lax.broadcasted_iota(jnp.int32, (1, LANES), 1)
    g = zs_ref[0][:, G_OFF:G_OFF + LANES] + bias_ref[:, G_OFF:G_OFF + LANES]
    is_f = (lane >= 2 * MLSTM_HEADS) & (lane < 4 * MLSTM_HEADS)
    lf_all = jnp.where(valid & is_f, -_softplus(-g), 0.0)
    r_i = lax.broadcasted_iota(jnp.int32, (L, L), 0)
    c_i = lax.broadcasted_iota(jnp.int32, (L, L), 1)
    sgn = 1 - 2 * d
    causal = (c_i - r_i) * sgn <= 0
    b_all = jnp.dot(causal.astype(F32), lf_all, precision=HI, preferred_element_type=F32)

    q = q_ref[0]
    k = k_ref[0] * (MLSTM_DQK ** -0.5)
    v = v_ref[0]
    heads = range(MLSTM_HEADS)
    bs, lis, dlogs = [], [], []
    for h in heads:
        fcol = 2 * MLSTM_HEADS + MLSTM_HEADS * d + h
        icol = MLSTM_HEADS * d + h
        b = jnp.sum(jnp.where(lane == fcol, b_all, 0.0), axis=1, keepdims=True)
        li = jnp.sum(jnp.where(lane == icol, g, 0.0), axis=1, keepdims=True)
        li = jnp.where(valid, li, NEG)
        fm = jnp.where(lane == 0, b, jnp.where(lane == 1, 1.0, 0.0))
        gm = jnp.where(lane == 0, 1.0, jnp.where(lane == 1, li - b, 0.0))
        bs.append(b)
        lis.append(li)
        dlogs.append(jnp.where(causal, _dot_nt(fm, gm, HI), NEG))
    m_sts = [m_sc[h][:, 0:1] for h in heads]
    avs = [b + m_st for b, m_st in zip(bs, m_sts)]
    mts = [jnp.maximum(a, jnp.max(dlog, axis=1, keepdims=True)) for a, dlog in zip(avs, dlogs)]
    qhs = [q[:, h * MLSTM_DQK:(h + 1) * MLSTM_DQK] for h in heads]
    khs = [k[:, h * MLSTM_DQK:(h + 1) * MLSTM_DQK] for h in heads]
    vhs = [v[:, h * MLSTM_DV:(h + 1) * MLSTM_DV] for h in heads]
    qbs = [x.astype(BF16) for x in qhs]
    vbs = [x.astype(BF16) for x in vhs]
    sqk = [_dot_nt(qb, kh.astype(BF16)) for qb, kh in zip(qbs, khs)]
    qcs = [jnp.dot(qb, c_sc[h].astype(BF16), preferred_element_type=F32) for h, qb in zip(heads, qbs)]
    ws = [jnp.exp(dlog - mt) * s for dlog, mt, s in zip(dlogs, mts, sqk)]
    wvs = [jnp.dot(w.astype(BF16), vb, preferred_element_type=F32) for w, vb in zip(ws, vbs)]
    for h in heads:
        inter = jnp.exp(avs[h] - mts[h])
        num = inter * qcs[h] + wvs[h]
        den = inter * jnp.sum(qhs[h] * n_sc[h], axis=1, keepdims=True) + jnp.sum(ws[h], axis=1, keepdims=True)
        o_ref[0, 0, :, h * MLSTM_DV:(h + 1) * MLSTM_DV] = num / jnp.maximum(jnp.abs(den), jnp.exp(-mts[h]))
    for h in heads:
        b, li, m_st = bs[h], lis[h], m_sts[h]
        bl = jnp.where(rev, b[0:1], b[L - 1:L])
        tail = bl - b + li
        m_new = jnp.maximum(bl + m_st, jnp.max(tail, axis=0, keepdims=True))
        kw = khs[h] * jnp.exp(tail - m_new)
        decay = jnp.exp(bl + m_st - m_new)
        c_sc[h] = decay * c_sc[h] + _dot_tn(kw, vhs[h])
        n_sc[h] = decay * n_sc[h] + jnp.sum(kw, axis=0, keepdims=True)
        m_sc[h] = jnp.broadcast_to(m_new, (1, LANES))


def _mlstm_scan(zm, zs, small_bias, lreal):
    n, tp, _ = zm.shape
    nc = tp // CHUNK
    cidx = lambda c, d: c + d * (nc - 1 - 2 * c)
    return pl.pallas_call(
        functools.partial(_mlstm_body, nc=nc, lreal=lreal),
        grid=(n, 2, nc),
        in_specs=[
            pl.BlockSpec((1, CHUNK, 512), lambda i, d, c: (i, cidx(c, d), C_QA // 512)),
            pl.BlockSpec((1, CHUNK, 512), lambda i, d, c: (i, cidx(c, d), C_KA // 512)),
            pl.BlockSpec((1, CHUNK, WIDTH), lambda i, d, c: (i, cidx(c, d), C_VA // WIDTH)),
            pl.BlockSpec((1, CHUNK, N_SMALL), lambda i, d, c: (i, cidx(c, d), 0)),
            pl.BlockSpec((1, N_SMALL), lambda i, d, c: (0, 0)),
        ],
        out_specs=pl.BlockSpec((1, 1, CHUNK, WIDTH), lambda i, d, c: (i, d, cidx(c, d), 0)),
        out_shape=jax.ShapeDtypeStruct((n, 2, tp, WIDTH), F32),
        scratch_shapes=[
            pltpu.VMEM((MLSTM_HEADS, MLSTM_DQK, MLSTM_DV), F32),
            pltpu.VMEM((MLSTM_HEADS, 1, MLSTM_DQK), F32),
            pltpu.VMEM((MLSTM_HEADS, 1, LANES), F32),
        ],
        compiler_params=_params(dimension_semantics=("parallel", "parallel", "arbitrary")),
        name="mlstm_scan",
    )(zm, zm, zm, zs, small_bias)


def _mlstm_post_body(h_ref, oa_ref, ga_ref, nw_ref, o_ref):
    h = h_ref[0, 0] + h_ref[0, 1]
    for j in range(MLSTM_HEADS):
        sl = slice(j * MLSTM_DV, (j + 1) * MLSTM_DV)
        hh = h[:, sl]
        hn = hh * lax.rsqrt(jnp.mean(hh * hh, axis=1, keepdims=True) + RMS_EPS) * nw_ref[:, sl]
        o_ref[0, :, sl] = (hn * _sigmoid(oa_ref[0][:, sl]) * _silu(ga_ref[0][:, sl])).astype(o_ref.dtype)


def _mlstm_post(hd, zm, norm_w):
    n, _, tp, _ = hd.shape
    tr = _tile(tp, 512, 16)
    return pl.pallas_call(
        _mlstm_post_body,
        grid=(n, tp // tr),
        in_specs=[
            pl.BlockSpec((1, 2, tr, WIDTH), lambda i, r: (i, 0, r, 0)),
            pl.BlockSpec((1, tr, WIDTH), lambda i, r: (i, r, C_OA // WIDTH)),
            pl.BlockSpec((1, tr, WIDTH), lambda i, r: (i, r, C_GA // WIDTH)),
            pl.BlockSpec((1, WIDTH), lambda i, r: (0, 0)),
        ],
        out_specs=pl.BlockSpec((1, tr, WIDTH), lambda i, r: (i, r, 0)),
        out_shape=jax.ShapeDtypeStruct((n, tp, WIDTH), BF16),
        compiler_params=_params(dimension_semantics=("parallel", "parallel")),
        name="mlstm_post",
    )(hd, zm, zm, norm_w.reshape(1, WIDTH))


def _cumsum_rows(tri_bf16, x):
    hi = x.astype(BF16)
    r1 = x - hi.astype(F32)
    mid = r1.astype(BF16)
    lo = (r1 - mid.astype(F32)).astype(BF16)
    dot = lambda t: jnp.dot(tri_bf16, t, preferred_element_type=F32)
    return dot(hi) + dot(mid) + dot(lo)


def _hgrn_body(q_ref, f_ref, v_ref, lb_ref, o_ref, st_sc, *, nc, lreal, rev):
    c = pl.program_id(1)
    cc = nc - 1 - c if rev else c

    @pl.when(c == 0)
    def _():
        st_sc[...] = jnp.zeros_like(st_sc)

    L = CHUNK
    sub = 8
    row = lax.broadcasted_iota(jnp.int32, (L, 1), 0)
    valid = (cc * L + row) < lreal
    r_i = lax.broadcasted_iota(jnp.int32, (L, L), 0)
    c_i = lax.broadcasted_iota(jnp.int32, (L, L), 1)
    tri = ((c_i >= r_i) if rev else (c_i <= r_i)).astype(BF16)
    ones = jnp.ones((HGRN_EXPAND, HGRN_EXPAND), BF16)

    nblk = L // HGRN_SUB
    vpb = HGRN_SUB // sub

    def group(gi, carry):
        hs = [gi * HGRN_GROUP + u for u in range(HGRN_GROUP)]
        sls = [pl.ds(pl.multiple_of(h * HGRN_EXPAND, HGRN_EXPAND), HGRN_EXPAND) for h in hs]
        qs, kks, vs, bs = [], [], [], []
        for sl in sls:
            fpre = f_ref[0, :, sl]
            lb = lb_ref[0, :, sl]
            log_f = jnp.where(valid, jnp.log(lb + (1.0 - lb) * _sigmoid(fpre)), 0.0)
            qs.append(_silu(q_ref[0, :, sl]))
            kks.append(jnp.where(valid, (1.0 - lb) * _sigmoid(-fpre), 0.0))
            vs.append(v_ref[0, :, sl])
            bs.append(_cumsum_rows(tri, log_f))
        accs = [[jnp.zeros((sub, HGRN_EXPAND), F32) for _ in range(L // sub)] for _ in hs]
        for blk in range(nblk):
            spans = []
            for s in range(blk * HGRN_SUB, (blk + 1) * HGRN_SUB):
                lo_v, hi_v = (blk * vpb, s // sub + 1) if rev else (s // sub, (blk + 1) * vpb)
                spans.append((s, lo_v, hi_v))
            reds = []
            for q, kk, b in zip(qs, kks, bs):
                pieces = []
                for s, lo_v, hi_v in spans:
                    rows = slice(lo_v * sub, hi_v * sub)
                    seen = (row[rows] <= s) if rev else (row[rows] >= s)
                    pieces.append((q[rows] * kk[s:s + 1]) * jnp.exp(jnp.where(seen, b[rows] - b[s:s + 1], NEG)))
                reds.append(jnp.dot(jnp.concatenate(pieces, axis=0).astype(BF16), ones, preferred_element_type=F32))
            for acc, red, v in zip(accs, reds, vs):
                r0 = 0
                for s, lo_v, hi_v in spans:
                    for jv in range(lo_v, hi_v):
                        acc[jv] = acc[jv] + red[r0:r0 + sub] * v[s:s + 1]
                        r0 += sub
        for blk in range(nblk):
            src = slice((blk + 1) * HGRN_SUB, L) if rev else slice(0, blk * HGRN_SUB)
            if src.start == src.stop:
                continue
            tgt = slice(blk * HGRN_SUB, (blk + 1) * HGRN_SUB)
            edge = (blk + 1) * HGRN_SUB if rev else blk * HGRN_SUB - 1
            atts = []
            for q, kk, b in zip(qs, kks, bs):
                beta = b[edge:edge + 1]
                qt = (q[tgt] * jnp.exp(b[tgt] - beta)).astype(BF16)
                kt = (kk[src] * jnp.exp(beta - b[src])).astype(BF16)
                atts.append(_dot_nt(qt, kt).astype(BF16))
            for acc, att, v in zip(accs, atts, vs):
                o_blk = jnp.dot(att, v[src].astype(BF16), preferred_element_type=F32)
                for jv in range(vpb):
                    acc[blk * vpb + jv] = acc[blk * vpb + jv] + o_blk[jv * sub:(jv + 1) * sub]
        sts = [st_sc[h] for h in hs]
        inters = [_dot_nt((q * jnp.exp(b)).astype(BF16), st.astype(BF16)) for q, b, st in zip(qs, bs, sts)]
        for sl, acc, inter in zip(sls, accs, inters):
            o_ref[0, :, sl] = jnp.concatenate(acc, axis=0) + inter
        for h, st, v, kk, b in zip(hs, sts, vs, kks, bs):
            bl = b[0:1] if rev else b[L - 1:L]
            st_sc[h] = st * jnp.exp(bl) + _dot_tn(v, kk * jnp.exp(bl - b))
        return carry

    lax.fori_loop(0, HGRN_HEADS // HGRN_GROUP, group, 0)


def _hgrn_scan(zm, lb, lreal, rev):
    n, tp, _ = zm.shape
    nc = tp // CHUNK
    cidx = (lambda c: nc - 1 - c) if rev else (lambda c: c)
    return pl.pallas_call(
        functools.partial(_hgrn_body, nc=nc, lreal=lreal, rev=rev),
        grid=(n, nc),
        in_specs=[
            pl.BlockSpec((1, CHUNK, WIDTH), lambda i, c: (i, cidx(c), C_QB // WIDTH)),
            pl.BlockSpec((1, CHUNK, WIDTH), lambda i, c: (i, cidx(c), C_FB // WIDTH + int(rev))),
            pl.BlockSpec((1, CHUNK, WIDTH), lambda i, c: (i, cidx(c), C_IB // WIDTH)),
            pl.BlockSpec((1, 1, WIDTH), lambda i, c: (int(rev), 0, 0)),
        ],
        out_specs=pl.BlockSpec((1, CHUNK, WIDTH), lambda i, c: (i, cidx(c), 0)),
        out_shape=jax.ShapeDtypeStruct((n, tp, WIDTH), F32),
        scratch_shapes=[pltpu.VMEM((HGRN_HEADS, HGRN_EXPAND, HGRN_EXPAND), F32)],
        compiler_params=_params(dimension_semantics=("parallel", "arbitrary")),
        name="hgrn_scan_bwd" if rev else "hgrn_scan_fwd",
    )(zm, zm, zm, lb)


def _hgrn_post_body(of_ref, ob_ref, gb_ref, nw_ref, y_ref):
    o = of_ref[0] + ob_ref[0]
    for j in range(HGRN_HEADS):
        sl = slice(j * HGRN_EXPAND, (j + 1) * HGRN_EXPAND)
        oh = o[:, sl]
        on = oh * lax.rsqrt(jnp.mean(oh * oh, axis=1, keepdims=True) + RMS_EPS) * nw_ref[...]
        y_ref[0, :, sl] = (on * _silu(gb_ref[0][:, sl])).astype(y_ref.dtype)


def _hgrn_post(of, ob, zm, norm_w):
    n, tp, _ = of.shape
    tr = _tile(tp, 512, 16)
    spec = pl.BlockSpec((1, tr, WIDTH), lambda i, r: (i, r, 0))
    return pl.pallas_call(
        _hgrn_post_body,
        grid=(n, tp // tr),
        in_specs=[spec, spec,
                  pl.BlockSpec((1, tr, WIDTH), lambda i, r: (i, r, C_GB // WIDTH)),
                  pl.BlockSpec((1, HGRN_EXPAND), lambda i, r: (0, 0))],
        out_specs=spec,
        out_shape=jax.ShapeDtypeStruct((n, tp, WIDTH), BF16),
        compiler_params=_params(dimension_semantics=("parallel", "parallel")),
        name="hgrn_post",
    )(of, ob, zm, norm_w.reshape(1, HGRN_EXPAND))


def _shift_body(x_ref, mu_ref, o_ref, *, lreal):
    x = x_ref[0]
    tp = x.shape[0]
    row = lax.broadcasted_iota(jnp.int32, (tp, 1), 0)
    prev = jnp.where(row == 0, 0.0, pltpu.roll(x, 1, 0))
    nxt = jnp.where(row + 1 >= lreal, 0.0, pltpu.roll(x, tp - 1, 0))
    o_ref[0] = x + mu_ref[0:1, :] * (prev - x) + mu_ref[1:2, :] * (nxt - x)


def _token_shift(z, col0, ncols, mu, lreal):
    n, tp, _ = z.shape
    tc = LANES
    return pl.pallas_call(
        functools.partial(_shift_body, lreal=lreal),
        grid=(n, ncols // tc),
        in_specs=[
            pl.BlockSpec((1, tp, tc), lambda i, j: (i, 0, col0 // tc + j)),
            pl.BlockSpec((2, tc), lambda i, j: (0, j)),
        ],
        out_specs=pl.BlockSpec((1, tp, tc), lambda i, j: (i, 0, j)),
        out_shape=jax.ShapeDtypeStruct((n, tp, ncols), F32),
        compiler_params=_params(dimension_semantics=("parallel", "parallel")),
        name="token_shift",
    )(z, mu)


def _seg_ones():
    r = lax.broadcasted_iota(jnp.int32, (LANES, LANES), 0) // RWKV_HEAD
    c = lax.broadcasted_iota(jnp.int32, (LANES, LANES), 1) // RWKV_HEAD
    return (r == c).astype(BF16)


def _head_sum(x, seg):
    hi = x.astype(BF16)
    r1 = x - hi.astype(F32)
    mid = r1.astype(BF16)
    lo = (r1 - mid.astype(F32)).astype(BF16)
    parts = []
    for j in range(WIDTH // LANES):
        sl = slice(j * LANES, (j + 1) * LANES)
        parts.append(jnp.dot(hi[:, sl], seg, preferred_element_type=F32)
                     + jnp.dot(mid[:, sl], seg, preferred_element_type=F32)
                     + jnp.dot(lo[:, sl], seg, preferred_element_type=F32))
    return jnp.concatenate(parts, axis=1)


def _rwkv_pre_body(f_ref, fs_ref, w2_ref, a2_ref, w0_ref, a0_ref, kk_ref, ka_ref, rk_ref,
                   r_o, v_o, kk_o, wf_o, wb_o, nf_o, nb_o, kf_o, kb_o, bonus_o, *, lreal):
    f = f_ref[0]
    r = f[:, 0:WIDTH]
    k = f[:, WIDTH:2 * WIDTH]
    v = f[:, 2 * WIDTH:3 * WIDTH]
    fs = fs_ref[0]
    wl = fs[:, 0:LANES]
    al = fs[:, LANES:2 * LANES]
    seg = _seg_ones()
    w = w0_ref[...] + jnp.dot(jnp.tanh(wl).astype(BF16), w2_ref[...], preferred_element_type=F32)
    wdec = jnp.exp(-jnp.exp(-_softplus(-w) - 0.5))
    a = _sigmoid(a0_ref[...] + jnp.dot(al.astype(BF16), a2_ref[...], preferred_element_type=F32))
    kk = k * kk_ref[...]
    kk = kk / jnp.maximum(jnp.sqrt(_head_sum(kk * kk, seg)), 1e-12)
    a_f, a_b = a[:, 0:WIDTH], a[:, WIDTH:2 * WIDTH]
    kd_f = k * (1.0 + (a_f - 1.0) * ka_ref[...])
    kd_b = k * (1.0 + (a_b - 1.0) * ka_ref[...])
    tr = f.shape[0]
    valid = (pl.program_id(1) * tr + lax.broadcasted_iota(jnp.int32, (tr, 1), 0)) < lreal
    keep = lambda x: jnp.where(valid, x, 0.0)

    def put(o_ref, x):
        for p in range(WIDTH // LANES):
            o_ref[0, :, p, :] = x[:, p * LANES:(p + 1) * LANES]

    put(r_o, keep(r))
    put(v_o, keep(v))
    put(kk_o, keep(kk))
    put(wf_o, jnp.where(valid, wdec[:, 0:WIDTH], 1.0))
    put(wb_o, jnp.where(valid, wdec[:, WIDTH:2 * WIDTH], 1.0))
    put(nf_o, keep(-(kk * a_f)))
    put(nb_o, keep(-(kk * a_b)))
    put(kf_o, keep(kd_f))
    put(kb_o, keep(kd_b))
    bonus_o[0] = _head_sum(r * (kd_f + kd_b) * rk_ref[...], seg) * v


def _rwkv_pre(fsh, fsm, w2cat, a2cat, w0, a0, k_k, k_a, r_k, lreal):
    n, tp, _ = fsh.shape
    tr = _tile(tp, 256, 8)
    row = lambda width: pl.BlockSpec((1, width), lambda i, r: (0, 0))
    out_spec = pl.BlockSpec((1, tr, WIDTH), lambda i, r: (i, r, 0))
    return pl.pallas_call(
        functools.partial(_rwkv_pre_body, lreal=lreal),
        grid=(n, tp // tr),
        in_specs=[
            pl.BlockSpec((1, tr, 3 * WIDTH), lambda i, r: (i, r, 0)),
            pl.BlockSpec((1, tr, 2 * LANES), lambda i, r: (i, r, 0)),
            pl.BlockSpec((LANES, 2 * WIDTH), lambda i, r: (0, 0)),
            pl.BlockSpec((LANES, 2 * WIDTH), lambda i, r: (0, 0)),
            row(2 * WIDTH), row(2 * WIDTH), row(WIDTH), row(WIDTH), row(WIDTH),
        ],
        out_specs=[pl.BlockSpec((1, tr, WIDTH // LANES, LANES), lambda i, r: (i, r, 0, 0))] * 9 + [out_spec],
        out_shape=[jax.ShapeDtypeStruct((n, tp, WIDTH // LANES, LANES), F32)] * 9
        + [jax.ShapeDtypeStruct((n, tp, WIDTH), F32)],
        compiler_params=_params(dimension_semantics=("parallel", "parallel")),
        name="rwkv_pre",
    )(fsh, fsm, w2cat, a2cat, w0, a0, k_k, k_a, r_k)


RWKV_PAIRS = WIDTH // LANES
RWKV_TB = 16
RWKV_NV1 = 8


def _sum_interleaved(terms, nacc):
    accs = []
    for i, t in enumerate(terms):
        if i < nacc:
            accs.append(t)
        else:
            accs[i % nacc] = accs[i % nacc] + t
    while len(accs) > 1:
        accs = [accs[i] + accs[i + 1] for i in range(0, len(accs), 2)]
    return accs[0]


def _rwkv_chain_step(s_sc, t_sc, load_v, g, nacc, nsplit=1):
    base = g * RWKV_HEAD
    row = lambda i, k: t_sc[i, base + k:base + k + 1, :]
    nv = s_sc.shape[2]
    part = nv // nsplit
    outs = []
    for h in range(nsplit):
        rs = slice(h * part, (h + 1) * part)
        sa = _sum_interleaved((s_sc[g, k, rs, :] * row(1, k) for k in range(RWKV_HEAD)), nacc)
        v_h = load_v()[rs]

        def update(k):
            s = s_sc[g, k, rs, :] * row(0, k) + sa * row(2, k) + v_h * row(3, k)
            s_sc[g, k, rs, :] = s
            return s * row(4, k)

        outs.append(_sum_interleaved((update(k) for k in range(RWKV_HEAD)), nacc))
    return outs[0] if nsplit == 1 else jnp.concatenate(outs, axis=0)


def _rwkv_pipelined_steps(build, compute, t_a, t_b):
    build(0, t_a)

    def pair(j2, carry):
        j = 2 * j2
        build(j + 1, t_b)
        compute(j, t_a)
        build(jnp.minimum(j + 2, RWKV_TB - 1), t_a)
        compute(j + 1, t_b)
        return carry

    lax.fori_loop(0, RWKV_TB // 2, pair, 0)


def _rwkv_scan8_body(wf, wb, kkf, kkb, nf, nb, kdf, kdb, rf, rb, vf, vb, yf_ref, yb_ref, s_sc, t_a, t_b):
    @pl.when(pl.program_id(0) == 0)
    def _():
        s_sc[...] = jnp.zeros_like(s_sc)

    nseq = wf.shape[0]
    operands = ((wf, wb), (kkf, kkb), (nf, nb), (kdf, kdb), (rf, rb), (vf, vb))

    def build(j, t_sc):
        jb = RWKV_TB - 1 - j
        for i, (xf_ref, xb_ref) in enumerate(operands):
            rows = [xf_ref[s, j] for s in range(nseq)] + [xb_ref[s, jb] for s in range(nseq)]
            t_sc[i] = jnp.concatenate(rows, axis=0).T

    def compute(j, t_sc):
        jb = RWKV_TB - 1 - j
        ys = [_rwkv_chain_step(s_sc, t_sc, lambda g=g: t_sc[5, g * RWKV_HEAD:(g + 1) * RWKV_HEAD, :], g, 4, 2)
              for g in range(2)]
        yt = jnp.concatenate(ys, axis=0).T
        for s in range(nseq):
            yf_ref[s, j] = yt[s * RWKV_PAIRS:(s + 1) * RWKV_PAIRS]
            yb_ref[s, jb] = yt[(nseq + s) * RWKV_PAIRS:(nseq + s + 1) * RWKV_PAIRS]

    _rwkv_pipelined_steps(build, compute, t_a, t_b)


def _rwkv_scan1_body(wf, wb, kkf, kkb, nf, nb, kdf, kdb, rf, rb, vf, vb, yf_ref, yb_ref, s_sc, t_a, t_b):
    @pl.when(pl.program_id(0) == 0)
    def _():
        s_sc[...] = jnp.zeros_like(s_sc)

    nv = RWKV_NV1
    noct = RWKV_HEAD // nv
    operands = ((wf, wb), (kkf, kkb), (nf, nb), (kdf, kdb), (rf, rb), (vf, vb))

    def build(j, t_sc):
        jb = RWKV_TB - 1 - j
        for i, (xf_ref, xb_ref) in enumerate(operands):
            rows = [jnp.broadcast_to(xf_ref[0, j, p:p + 1, :], (8, LANES)) for p in range(RWKV_PAIRS)]
            rows += [jnp.broadcast_to(xb_ref[0, jb, p:p + 1, :], (8, LANES)) for p in range(RWKV_PAIRS)]
            t_sc[i] = jnp.concatenate(rows, axis=0).T

    def compute(j, t_sc):
        jb = RWKV_TB - 1 - j
        octet = lax.broadcasted_iota(jnp.int32, (nv, LANES), 1) % noct
        ys = []
        for g in range(2):
            v_g = jnp.zeros((nv, LANES), F32)
            for q in range(noct):
                r0 = g * RWKV_HEAD + q * nv
                v_g = jnp.where(octet == q, t_sc[5, r0:r0 + nv, :], v_g)
            ys.append(_rwkv_chain_step(s_sc, t_sc, lambda v_g=v_g: v_g, g, 8))
        zrows = [jnp.where(octet == q, ys[g], 0.0) for g in range(2) for q in range(noct)]
        yt = jnp.concatenate(zrows, axis=0).T
        for dp in range(2 * RWKV_PAIRS):
            rowv = jnp.sum(yt[dp * 8:(dp + 1) * 8], axis=0, keepdims=True)
            p = dp % RWKV_PAIRS
            if dp < RWKV_PAIRS:
                yf_ref[0, j, p:p + 1, :] = rowv
            else:
                yb_ref[0, jb, p:p + 1, :] = rowv

    _rwkv_pipelined_steps(build, compute, t_a, t_b)


def _rwkv_scan(wf, wb, kk, nf, nb, kdf, kdb, r, v):
    n, tp = r.shape[:2]
    nblk = tp // RWKV_TB
    fspec = pl.BlockSpec((n, RWKV_TB, RWKV_PAIRS, LANES), lambda i: (0, i, 0, 0))
    bspec = pl.BlockSpec((n, RWKV_TB, RWKV_PAIRS, LANES), lambda i: (0, nblk - 1 - i, 0, 0))
    if n == 8:
        body, nv = _rwkv_scan8_body, RWKV_HEAD
    else:
        assert n == 1
        body, nv = _rwkv_scan1_body, RWKV_NV1
    return pl.pallas_call(
        body,
        grid=(nblk,),
        in_specs=[fspec, bspec] * 6,
        out_specs=[fspec, bspec],
        out_shape=[jax.ShapeDtypeStruct((n, tp, RWKV_PAIRS, LANES), F32)] * 2,
        scratch_shapes=[pltpu.VMEM((2, RWKV_HEAD, nv, LANES), F32),
                        pltpu.VMEM((6, LANES, LANES), F32), pltpu.VMEM((6, LANES, LANES), F32)],
        compiler_params=_params(dimension_semantics=("arbitrary",)),
        name="rwkv_scan",
    )(wf, wb, kk, kk, nf, nb, kdf, kdb, r, r, v, v)


def _rwkv_post_body(yf_ref, yb_ref, bonus_ref, gc_ref, lnw_ref, lnb_ref, o_ref):
    seg = _seg_ones()
    y = jnp.concatenate([yf_ref[0, :, p, :] + yb_ref[0, :, p, :] for p in range(RWKV_PAIRS)], axis=1)
    mean = _head_sum(y, seg) * (1.0 / RWKV_HEAD)
    yc = y - mean
    var = _head_sum(yc * yc, seg) * (1.0 / RWKV_HEAD)
    yn = yc * lax.rsqrt(var + RWKV_GN_EPS) * lnw_ref[...] + lnb_ref[...]
    o_ref[0] = ((yn + bonus_ref[0]) * _silu(gc_ref[0])).astype(o_ref.dtype)


def _rwkv_post(yf, yb, bonus, zm, ln_w, ln_b):
    n, tp = yf.shape[:2]
    tr = _tile(tp, 512, 16)
    spec = pl.BlockSpec((1, tr, WIDTH), lambda i, r: (i, r, 0))
    yspec = pl.BlockSpec((1, tr, RWKV_PAIRS, LANES), lambda i, r: (i, r, 0, 0))
    row = pl.BlockSpec((1, WIDTH), lambda i, r: (0, 0))
    return pl.pallas_call(
        _rwkv_post_body,
        grid=(n, tp // tr),
        in_specs=[yspec, yspec, spec, pl.BlockSpec((1, tr, WIDTH), lambda i, r: (i, r, C_GC // WIDTH)), row, row],
        out_specs=spec,
        out_shape=jax.ShapeDtypeStruct((n, tp, WIDTH), BF16),
        compiler_params=_params(dimension_semantics=("parallel", "parallel")),
        name="rwkv_post",
    )(yf, yb, bonus, zm, ln_w.reshape(1, WIDTH), ln_b.reshape(1, WIDTH))


def _rwkv_branch(zm, zs, p, lreal):
    fsh = _token_shift(zm, C_RC, 3 * WIDTH, p['mu_main'], lreal)
    fsm = _token_shift(zs, 0, 2 * LANES, p['mu_small'], lreal)
    r, v, kk, wf, wb, nf, nb, kf, kb, bonus = _rwkv_pre(
        fsh, fsm, p['w2cat'], p['a2cat'], p['w0'], p['a0'], p['k_k'], p['k_a'], p['r_k'], lreal)
    yf, yb = _rwkv_scan(wf, wb, kk, nf, nb, kf, kb, r, v)
    return _rwkv_post(yf, yb, bonus, zm, p['ln_w'], p['ln_b'])


def _s5_core_body(u_ref, m_ref, n_ref, q_ref, pw_ref, cidx_ref, y_ref, *, nsteps, nvalid, nblocks):
    cidx = cidx_ref[...]
    u = jnp.where(cidx < nvalid, u_ref[0], 0.0).astype(BF16)
    rows = u.shape[0]
    y = None
    for d in range(2):
        yd = jnp.dot(u, m_ref[d, 0], preferred_element_type=F32)
        x = jnp.dot(u, n_ref[d, 0], preferred_element_type=F32)
        for j in range(nsteps):
            s = 1 << j
            if d == 0:
                xs = jnp.where(cidx >= s, pltpu.roll(x, s, 0), 0.0)
            else:
                xs = jnp.where(cidx + s < nblocks, pltpu.roll(x, rows - s, 0), 0.0)
            x = (x + xs * pw_ref[d, 0, 2 * j:2 * j + 1, :]
                 + pltpu.roll(xs, S5_STATE, 1) * pw_ref[d, 0, 2 * j + 1:2 * j + 2, :])
        if d == 0:
            xin = jnp.where(cidx >= 1, pltpu.roll(x, 1, 0), 0.0)
        else:
            xin = jnp.where(cidx + 1 < nblocks, pltpu.roll(x, rows - 1, 0), 0.0)
        yd = yd + jnp.dot(xin.astype(BF16), q_ref[d, 0], preferred_element_type=F32)
        y = yd if y is None else y + yd
    y_ref[0] = y


def _s5_core(u, mats, cidx, nsteps, nvalid, nblocks):
    g, rows, _ = u.shape
    mm, nn, qq, pw = mats
    return pl.pallas_call(
        functools.partial(_s5_core_body, nsteps=nsteps, nvalid=nvalid, nblocks=nblocks),
        grid=(g,),
        in_specs=[
            pl.BlockSpec((1, rows, S5_ROW), lambda j: (j, 0, 0)),
            pl.BlockSpec((2, 1, S5_ROW, S5_ROW), lambda j: (0, j, 0, 0)),
            pl.BlockSpec((2, 1, S5_ROW, 2 * S5_STATE), lambda j: (0, j, 0, 0)),
            pl.BlockSpec((2, 1, 2 * S5_STATE, S5_ROW), lambda j: (0, j, 0, 0)),
            pl.BlockSpec((2, 1, pw.shape[2], 2 * S5_STATE), lambda j: (0, j, 0, 0)),
            pl.BlockSpec((rows, 1), lambda j: (0, 0)),
        ],
        out_specs=pl.BlockSpec((1, rows, S5_ROW), lambda j: (j, 0, 0)),
        out_shape=jax.ShapeDtypeStruct((g, rows, S5_ROW), F32),
        compiler_params=_params(dimension_semantics=("parallel",)),
        name="s5_core",
    )(u, mm, nn, qq, pw, cidx)


def _s5_post_body(y_ref, ud_ref, gd_ref, d_ref, w_ref, b_ref, o_ref):
    y = y_ref[0] + d_ref[...] * ud_ref[0]
    g = 0.5 * y * (1.0 + jnp.tanh(0.7978845608028654 * (y + 0.044715 * (y * y * y))))
    glu = jnp.dot(g.astype(BF16), w_ref[...], preferred_element_type=F32) + b_ref[...]
    o_ref[0] = (g * _sigmoid(glu) * _silu(gd_ref[0])).astype(o_ref.dtype)


def _s5_post(ys, zm, d, glu_w, glu_b):
    n, tp, _ = ys.shape
    tr = _tile(tp, 512, 16)
    spec = pl.BlockSpec((1, tr, WIDTH), lambda i, r: (i, r, 0))
    row = pl.BlockSpec((1, WIDTH), lambda i, r: (0, 0))
    return pl.pallas_call(
        _s5_post_body,
        grid=(n, tp // tr),
        in_specs=[spec,
                  pl.BlockSpec((1, tr, WIDTH), lambda i, r: (i, r, C_UD // WIDTH)),
                  pl.BlockSpec((1, tr, WIDTH), lambda i, r: (i, r, C_GD // WIDTH)),
                  row, pl.BlockSpec((WIDTH, WIDTH), lambda i, r: (0, 0)), row],
        out_specs=spec,
        out_shape=jax.ShapeDtypeStruct((n, tp, WIDTH), BF16),
        compiler_params=_params(dimension_semantics=("parallel", "parallel")),
        name="s5_post",
    )(ys, zm, zm, d.reshape(1, WIDTH), glu_w, glu_b.reshape(1, WIDTH))


def _s5_mats(a_re, a_im, log_dt, b_re, b_im, c_re, c_im, max_blocks):
    dt = jnp.exp(log_dt)[..., None]
    mag = jnp.exp(a_re * dt)
    ang = a_im * dt
    ab_re, ab_im = mag * jnp.cos(ang), mag * jnp.sin(ang)
    den = a_re * a_re + a_im * a_im
    xr, yi = ab_re - 1.0, ab_im
    coef_re = (xr * a_re + yi * a_im) / den
    coef_im = (yi * a_re - xr * a_im) / den
    bb_re = coef_re[..., None] * b_re - coef_im[..., None] * b_im
    bb_im = coef_re[..., None] * b_im + coef_im[..., None] * b_re
    pr, pi = [jnp.ones_like(ab_re)], [jnp.zeros_like(ab_re)]
    for _ in range(S5_BLOCK):
        pr, pi = pr + [pr[-1] * ab_re - pi[-1] * ab_im], pi + [pr[-1] * ab_im + pi[-1] * ab_re]
    pw_re, pw_im = jnp.stack(pr), jnp.stack(pi)
    t_re = pw_re[..., None] * bb_re - pw_im[..., None] * bb_im
    t_im = pw_re[..., None] * bb_im + pw_im[..., None] * bb_re
    kj = (jnp.einsum('dgcp,jdgpe->jdgce', c_re, t_re, precision=HI)
          - jnp.einsum('dgcp,jdgpe->jdgce', c_im, t_im, precision=HI))
    s_i = jnp.arange(S5_BLOCK)[:, None]
    i_i = jnp.arange(S5_BLOCK)[None, :]
    rev = jnp.arange(S5_BLOCK - 1, -1, -1)

    def tables(d):
        dist = (s_i - i_i) if d else (i_i - s_i)
        m6 = jnp.where((dist >= 0)[:, :, None, None, None], kj[jnp.clip(dist, 0, S5_BLOCK), d], 0.0)
        mm_d = jnp.transpose(m6, (2, 0, 4, 1, 3)).reshape(S5_GROUPS, S5_ROW, S5_ROW)
        order = jnp.arange(S5_BLOCK) if d else rev
        n_re = jnp.transpose(t_re[order, d], (1, 0, 3, 2)).reshape(S5_GROUPS, S5_ROW, S5_STATE)
        n_im = jnp.transpose(t_im[order, d], (1, 0, 3, 2)).reshape(S5_GROUPS, S5_ROW, S5_STATE)
        steps_in = (rev + 1) if d else (jnp.arange(S5_BLOCK) + 1)
        pr_i, pi_i = pw_re[steps_in, d][:, :, None, :], pw_im[steps_in, d][:, :, None, :]
        ca_re = c_re[d][None] * pr_i - c_im[d][None] * pi_i
        ca_im = c_re[d][None] * pi_i + c_im[d][None] * pr_i
        q_re = jnp.transpose(ca_re, (1, 3, 0, 2)).reshape(S5_GROUPS, S5_STATE, S5_ROW)
        q_im = jnp.transpose(-ca_im, (1, 3, 0, 2)).reshape(S5_GROUPS, S5_STATE, S5_ROW)
        return mm_d, jnp.concatenate([n_re, n_im], axis=-1), jnp.concatenate([q_re, q_im], axis=1)

    mm, nn, qq = (jnp.stack(t).astype(BF16) for t in zip(tables(0), tables(1)))
    sr, si = pw_re[S5_BLOCK], pw_im[S5_BLOCK]
    rows = []
    steps = 0
    while (1 << steps) < max_blocks:
        rows += [jnp.concatenate([sr, sr], axis=-1), jnp.concatenate([-si, si], axis=-1)]
        sr, si = sr * sr - si * si, 2.0 * sr * si
        steps += 1
    while len(rows) % 8:
        rows.append(jnp.zeros_like(rows[0]))
    pw = jnp.stack(rows, axis=2)
    return (mm, nn, qq, pw), steps


def _s5_branch(zm, p, lreal):
    n, tp, _ = zm.shape
    assert lreal % S5_BLOCK == 0 and tp % S5_BLOCK == 0
    nvalid = lreal // S5_BLOCK
    nb = tp // S5_BLOCK
    rows = n * nb
    rows_p = -(-rows // 8) * 8
    u = zm[:, :, C_UD:C_UD + WIDTH].reshape(n, nb, S5_BLOCK, S5_GROUPS, S5_GROUP)
    u = jnp.transpose(u, (3, 0, 1, 2, 4)).reshape(S5_GROUPS, rows, S5_ROW)
    u = jnp.pad(u, ((0, 0), (0, rows_p - rows), (0, 0)))
    cidx = jnp.pad(jnp.tile(jnp.arange(nb, dtype=jnp.int32), n), (0, rows_p - rows),
                   constant_values=nb).reshape(rows_p, 1)
    mats, nsteps = p['s5_mats'](nb)
    y = _s5_core(u, mats, cidx, nsteps, nvalid, nb)[:, :rows]
    ys = jnp.transpose(y.reshape(S5_GROUPS, n, nb, S5_BLOCK, S5_GROUP), (1, 2, 3, 0, 4)).reshape(n, tp, WIDTH)
    return _s5_post(ys, zm, p['s5_d'], p['glu_w'], p['glu_b'])


def _merge_body(h_ref, ya_ref, yb_ref, yc_ref, yd_ref, wg_ref, bg_ref, wb_ref, o_ref):
    h = h_ref[...]
    acc = None
    for b, y_ref in enumerate((ya_ref, yb_ref, yc_ref, yd_ref)):
        gate = _sigmoid(jnp.dot(h, wg_ref[b], preferred_element_type=F32) + bg_ref[b])
        term = gate * jnp.dot(y_ref[...], wb_ref[b], preferred_element_type=F32)
        acc = term if acc is None else acc + term
    o_ref[...] = acc.astype(o_ref.dtype)


def _merge(h, ys, wg, bg, wb):
    m, d = h.shape
    tm = _tile(m, 1024, 16)
    tn = 256
    yspec = pl.BlockSpec((tm, WIDTH), lambda i, j: (i, 0))
    return pl.pallas_call(
        _merge_body,
        grid=(m // tm, d // tn),
        in_specs=[pl.BlockSpec((tm, d), lambda i, j: (i, 0)), yspec, yspec, yspec, yspec,
                  pl.BlockSpec((4, d, tn), lambda i, j: (0, 0, j)),
                  pl.BlockSpec((4, 1, tn), lambda i, j: (0, 0, j)),
                  pl.BlockSpec((4, WIDTH, tn), lambda i, j: (0, 0, j))],
        out_specs=pl.BlockSpec((tm, tn), lambda i, j: (i, j)),
        out_shape=jax.ShapeDtypeStruct((m, d), BF16),
        compiler_params=_params(dimension_semantics=("parallel", "parallel")),
        name="merge",
    )(h, *ys, wg, bg, wb)


def _layer(xp, p, lreal):
    n, tp, d = xp.shape
    m = n * tp
    x2 = xp.reshape(m, d)
    h = _rmsnorm(x2, p['norm_w'], BF16)
    zm = _mm(h, p['w_main'], name="proj_main").reshape(n, tp, N_MAIN)
    zs = _mm(h, p['w_small'], name="proj_small").reshape(n, tp, N_SMALL)
    ya = _mlstm_post(_mlstm_scan(zm, zs, p['small_bias'], lreal), zm, p['mlstm_norm_w'])
    yb = _hgrn_post(_hgrn_scan(zm, p['hgrn_lb'], lreal, False), _hgrn_scan(zm, p['hgrn_lb'], lreal, True), zm,
                    p['hgrn_norm_w'])
    yc = _rwkv_branch(zm, zs, p, lreal)
    yd = _s5_branch(zm, p, lreal)
    ys = [y.reshape(m, WIDTH) for y in (ya, yb, yc, yd)]
    merged = _merge(h, ys, p['w_gate'], p['b_gate'], p['w_branch'])
    return _mm(merged, p['w_out'], res=x2, name="proj_out").reshape(n, tp, d)


def _trunk(x, meta, layers, final_norm_w):
    n, t, d = x.shape
    lreal = t + N_META
    tp = lreal + TAIL_PAD
    xp = jnp.concatenate([jnp.broadcast_to(meta[None], (n, N_META, d)), x, jnp.zeros((n, TAIL_PAD, d), x.dtype)], axis=1)
    for p in layers:
        xp = _layer(xp, p, lreal)
    y = _rmsnorm(xp.reshape(n * tp, d), final_norm_w, F32).reshape(n, tp, d)
    return y[:, N_META:lreal]


def _layer_params(l, a):
    w_in = a['w_in'][l]
    offs = {}
    acc = 0
    sizes = (512, 512, WIDTH, WIDTH, 8, 8, WIDTH, WIDTH, 2 * WIDTH, WIDTH, WIDTH, WIDTH, WIDTH, WIDTH,
             2 * RWKV_LORA, 2 * RWKV_LORA, WIDTH, WIDTH, WIDTH)
    names = ('qa', 'ka', 'va', 'oa', 'iga', 'fga', 'ga', 'qb', 'fb', 'ib', 'gb', 'rc', 'kc', 'vc', 'wlc', 'alc',
             'gc', 'ud', 'gd')
    for nm, sz in zip(names, sizes):
        offs[nm] = (acc, acc + sz)
        acc += sz
    col = lambda nm: w_in[:, offs[nm][0]:offs[nm][1]]
    main_order = ('qa', 'ka', 'va', 'oa', 'ga', 'qb', 'fb', 'ib', 'gb', 'rc', 'kc', 'vc', 'gc', 'ud', 'gd')
    w_main = jnp.concatenate([col(nm) for nm in main_order], axis=1).astype(BF16)
    w_small = jnp.concatenate([col('wlc'), col('alc'), col('iga'), col('fga'),
                               jnp.zeros((D_MODEL, LANES - 16), F32)], axis=1).astype(BF16)
    small_bias = jnp.concatenate([jnp.zeros((G_OFF,), F32), a['mlstm_ig_b'][l].reshape(-1),
                                  a['mlstm_fg_b'][l].reshape(-1), jnp.zeros((LANES - 16,), F32)]).reshape(1, N_SMALL)
    lbw = jax.nn.softmax(a['hgrn_lower_bounds'].astype(F32), axis=1)
    hgrn_lb = (jnp.cumsum(lbw, axis=1) - lbw[:, :1])[:, l].reshape(2, 1, WIDTH)
    mu = a['rwkv_shift_mu'][l]
    zero = jnp.zeros((RWKV_LORA, WIDTH), F32)
    blockdiag = lambda w2: jnp.concatenate([jnp.concatenate([w2[0], zero], axis=1),
                                            jnp.concatenate([zero, w2[1]], axis=1)], axis=0)
    s5_args = tuple(a[k][l].astype(F32) for k in ('s5_a_re', 's5_a_im', 's5_log_dt', 's5_b_re', 's5_b_im',
                                                  's5_c_re', 's5_c_im'))
    return {
        'norm_w': a['norm_w'][l], 'w_main': w_main, 'w_small': w_small, 'small_bias': small_bias,
        'mlstm_norm_w': a['mlstm_norm_w'][l], 'hgrn_lb': hgrn_lb, 'hgrn_norm_w': a['hgrn_norm_w'][l],
        'mu_main': mu[:, :3 * WIDTH], 'mu_small': mu[:, 3 * WIDTH:],
        'w2cat': blockdiag(a['rwkv_w2'][l]).astype(BF16), 'a2cat': blockdiag(a['rwkv_a2'][l]).astype(BF16),
        'w0': a['rwkv_w0'][l].reshape(1, 2 * WIDTH), 'a0': a['rwkv_a0'][l].reshape(1, 2 * WIDTH),
        'k_k': a['rwkv_k_k'][l].reshape(1, WIDTH), 'k_a': a['rwkv_k_a'][l].reshape(1, WIDTH),
        'r_k': a['rwkv_r_k'][l].reshape(1, WIDTH),
        'ln_w': a['rwkv_ln_w'][l], 'ln_b': a['rwkv_ln_b'][l],
        's5_mats': functools.partial(_s5_mats, *s5_args),
        's5_d': a['s5_d'][l], 'glu_w': a['s5_glu_w'][l].astype(BF16), 'glu_b': a['s5_glu_b'][l],
        'w_gate': a['w_gate'][l].astype(BF16), 'b_gate': a['b_gate'][l].reshape(4, 1, D_MODEL),
        'w_branch': a['w_branch'][l].astype(BF16), 'w_out': a['w_out'][l].astype(BF16),
    }


def kernel(x_prompt, x_sample, meta_tokens, norm_w, w_in, mlstm_ig_b, mlstm_fg_b, mlstm_norm_w, hgrn_lower_bounds, hgrn_norm_w, rwkv_shift_mu, rwkv_w0, rwkv_w2, rwkv_a0, rwkv_a2, rwkv_k_k, rwkv_k_a, rwkv_r_k, rwkv_ln_w, rwkv_ln_b, s5_a_re, s5_a_im, s5_log_dt, s5_b_re, s5_b_im, s5_c_re, s5_c_im, s5_d, s5_glu_w, s5_glu_b, w_branch, w_gate, b_gate, w_out, final_norm_w):
    a = dict(norm_w=norm_w, w_in=w_in, mlstm_ig_b=mlstm_ig_b, mlstm_fg_b=mlstm_fg_b, mlstm_norm_w=mlstm_norm_w,
             hgrn_lower_bounds=hgrn_lower_bounds, hgrn_norm_w=hgrn_norm_w, rwkv_shift_mu=rwkv_shift_mu,
             rwkv_w0=rwkv_w0, rwkv_w2=rwkv_w2, rwkv_a0=rwkv_a0, rwkv_a2=rwkv_a2, rwkv_k_k=rwkv_k_k,
             rwkv_k_a=rwkv_k_a, rwkv_r_k=rwkv_r_k, rwkv_ln_w=rwkv_ln_w, rwkv_ln_b=rwkv_ln_b, s5_a_re=s5_a_re,
             s5_a_im=s5_a_im, s5_log_dt=s5_log_dt, s5_b_re=s5_b_re, s5_b_im=s5_b_im, s5_c_re=s5_c_re,
             s5_c_im=s5_c_im, s5_d=s5_d, s5_glu_w=s5_glu_w, s5_glu_b=s5_glu_b, w_branch=w_branch, w_gate=w_gate,
             b_gate=b_gate, w_out=w_out)
    layers = [_layer_params(l, a) for l in range(DEPTH)]
    meta = meta_tokens.astype(x_prompt.dtype)
    return (_trunk(x_prompt, meta, layers, final_norm_w), _trunk(x_sample, meta, layers, final_norm_w))
```

```python
import functools

import jax
import jax.numpy as jnp
from jax import lax
from jax.experimental import pallas as pl
from jax.experimental.pallas import tpu as pltpu

F32 = jnp.float32
BF16 = jnp.bfloat16
HI = lax.Precision.HIGHEST

D_MODEL = 2048
DEPTH = 2
N_META = 16
CHUNK = 64
TAIL_PAD = CHUNK - N_META
WIDTH = D_MODEL // 2
RMS_EPS = 1e-6
NEG = -1e30

MLSTM_HEADS = 4
MLSTM_DV = WIDTH // MLSTM_HEADS
MLSTM_DQK = MLSTM_DV // 2
HGRN_EXPAND = 128
HGRN_HEADS = WIDTH // HGRN_EXPAND
HGRN_SUB = 16
HGRN_GROUP = 8
RWKV_HEAD = 64
RWKV_HEADS = WIDTH // RWKV_HEAD
RWKV_LORA = 64
RWKV_GN_EPS = 64e-5
S5_GROUP = 16
S5_GROUPS = WIDTH // S5_GROUP
S5_STATE = 64
S5_BLOCK = 16
S5_ROW = S5_BLOCK * S5_GROUP

LANES = 128
VMEM_LIMIT = 48 * 1024 * 1024

C_QA, C_KA, C_VA, C_OA, C_GA = 0, 512, 1024, 2048, 3072
C_QB, C_FB, C_IB, C_GB = 4096, 5120, 7168, 8192
C_RC, C_KC, C_VC, C_GC = 9216, 10240, 11264, 12288
C_UD, C_GD = 13312, 14336
N_MAIN = 15360
N_SMALL = 384
G_OFF = 256


def _params(**kw):
    return pltpu.CompilerParams(vmem_limit_bytes=VMEM_LIMIT, **kw)


def _tile(total, cap, mult):
    best = None
    for t in range(mult, min(total, cap) + 1, mult):
        if total % t == 0:
            best = t
    assert best is not None, (total, cap, mult)
    return best


def _sigmoid(x):
    return jax.nn.sigmoid(x)


def _silu(x):
    return x * jax.nn.sigmoid(x)


def _softplus(x):
    return jnp.maximum(x, 0.0) + jnp.log(1.0 + jnp.exp(-jnp.abs(x)))


def _dot_nt(a, b, precision=None):
    return lax.dot_general(a, b, (((1,), (1,)), ((), ())), precision=precision, preferred_element_type=F32)


def _dot_tn(a, b, precision=None):
    return lax.dot_general(a, b, (((0,), (0,)), ((), ())), precision=precision, preferred_element_type=F32)


def _rmsnorm_body(x_ref, w_ref, o_ref):
    x = x_ref[...]
    y = x * lax.rsqrt(jnp.mean(x * x, axis=-1, keepdims=True) + RMS_EPS) * w_ref[...]
    o_ref[...] = y.astype(o_ref.dtype)


def _rmsnorm(x2d, w, out_dtype):
    m, d = x2d.shape
    tm = _tile(m, 512, 16)
    return pl.pallas_call(
        _rmsnorm_body,
        grid=(m // tm,),
        in_specs=[pl.BlockSpec((tm, d), lambda i: (i, 0)), pl.BlockSpec((1, d), lambda i: (0, 0))],
        out_specs=pl.BlockSpec((tm, d), lambda i: (i, 0)),
        out_shape=jax.ShapeDtypeStruct((m, d), out_dtype),
        compiler_params=_params(dimension_semantics=("parallel",)),
        name="rmsnorm",
    )(x2d, w.reshape(1, d).astype(F32))


def _mm_body(*refs, has_res):
    x_ref, w_ref = refs[0], refs[1]
    o_ref = refs[-1]
    acc = jnp.dot(x_ref[...], w_ref[...], preferred_element_type=F32)
    if has_res:
        acc = refs[2][...] + acc
    o_ref[...] = acc.astype(o_ref.dtype)


def _mm(x, w, res=None, out_dtype=F32, name="mm"):
    m, k = x.shape
    n = w.shape[1]
    tm = _tile(m, 1408, 16)
    tn = 1024 if n % 1024 == 0 else n
    in_specs = [pl.BlockSpec((tm, k), lambda i, j: (i, 0)), pl.BlockSpec((k, tn), lambda i, j: (0, j))]
    args = [x, w]
    if res is not None:
        in_specs.append(pl.BlockSpec((tm, tn), lambda i, j: (i, j)))
        args.append(res)
    return pl.pallas_call(
        functools.partial(_mm_body, has_res=res is not None),
        grid=(m // tm, n // tn),
        in_specs=in_specs,
        out_specs=pl.BlockSpec((tm, tn), lambda i, j: (i, j)),
        out_shape=jax.ShapeDtypeStruct((m, n), out_dtype),
        compiler_params=_params(dimension_semantics=("parallel", "parallel")),
        name=name,
    )(*args)


def _mlstm_body(qf_ref, kf_ref, vf_ref, zf_ref, qb_ref, kb_ref, vb_ref, zb_ref, bias_ref, of_ref, ob_ref,
                c_sc, n_sc, m_sc, *, nc, lreal):
    c = pl.program_id(1)

    @pl.when(c == 0)
    def _():
        c_sc[...] = jnp.zeros_like(c_sc)
        n_sc[...] = jnp.zeros_like(n_sc)
        m_sc[...] = jnp.zeros_like(m_sc)

    L = CHUNK
    row = lax.broadcasted_iota(jnp.int32, (L, 1), 0)
    lane = lax.broadcasted_iota(jnp.int32, (1, LANES), 1)
    is_f = (lane >= 2 * MLSTM_HEADS) & (lane < 4 * MLSTM_HEADS)
    r_i = lax.broadcasted_iota(jnp.int32, (L, L), 0)
    c_i = lax.broadcasted_iota(jnp.int32, (L, L), 1)
    valids, gs, causals, balls, qd, kd, vd = [], [], [], [], [], [], []
    for d, (q_ref, k_ref, v_ref, zs_ref) in enumerate(((qf_ref, kf_ref, vf_ref, zf_ref),
                                                       (qb_ref, kb_ref, vb_ref, zb_ref))):
        cc = nc - 1 - c if d else c
        valid = (cc * L + row) < lreal
        g = zs_ref[0][:, G_OFF:G_OFF + LANES] + bias_ref[:, G_OFF:G_OFF + LANES]
        lf_all = jnp.where(valid & is_f, -_softplus(-g), 0.0)
        causal = (c_i >= r_i) if d else (c_i <= r_i)
        valids.append(valid)
        gs.append(g)
        causals.append(causal)
        balls.append(jnp.dot(causal.astype(F32), lf_all, precision=HI, preferred_element_type=F32))
        qd.append(q_ref[0])
        kd.append(k_ref[0] * (MLSTM_DQK ** -0.5))
        vd.append(v_ref[0])
    o_refs = (of_ref, ob_ref)
    chains = [(d, h) for d in range(2) for h in range(MLSTM_HEADS)]
    bs, lis, dlogs = [], [], []
    for d, h in chains:
        fcol = 2 * MLSTM_HEADS + MLSTM_HEADS * d + h
        icol = MLSTM_HEADS * d + h
        b = jnp.sum(jnp.where(lane == fcol, balls[d], 0.0), axis=1, keepdims=True)
        li = jnp.sum(jnp.where(lane == icol, gs[d], 0.0), axis=1, keepdims=True)
        li = jnp.where(valids[d], li, NEG)
        fm = jnp.where(lane == 0, b, jnp.where(lane == 1, 1.0, 0.0))
        gm = jnp.where(lane == 0, 1.0, jnp.where(lane == 1, li - b, 0.0))
        bs.append(b)
        lis.append(li)
        dlogs.append(jnp.where(causals[d], _dot_nt(fm, gm, HI), NEG))
    m_sts = [m_sc[d, h][:, 0:1] for d, h in chains]
    avs = [b + m_st for b, m_st in zip(bs, m_sts)]
    mts = [jnp.maximum(a, jnp.max(dlog, axis=1, keepdims=True)) for a, dlog in zip(avs, dlogs)]
    qhs = [qd[d][:, h * MLSTM_DQK:(h + 1) * MLSTM_DQK] for d, h in chains]
    khs = [kd[d][:, h * MLSTM_DQK:(h + 1) * MLSTM_DQK] for d, h in chains]
    vhs = [vd[d][:, h * MLSTM_DV:(h + 1) * MLSTM_DV] for d, h in chains]
    qbs = [x.astype(BF16) for x in qhs]
    vbs = [x.astype(BF16) for x in vhs]
    sqk = [_dot_nt(qb, kh.astype(BF16)) for qb, kh in zip(qbs, khs)]
    qcs = [jnp.dot(qb, c_sc[d, h].astype(BF16), preferred_element_type=F32) for (d, h), qb in zip(chains, qbs)]
    ws = [jnp.exp(dlog - mt) * s for dlog, mt, s in zip(dlogs, mts, sqk)]
    wvs = [jnp.dot(w.astype(BF16), vb, preferred_element_type=F32) for w, vb in zip(ws, vbs)]
    for e, (d, h) in enumerate(chains):
        inter = jnp.exp(avs[e] - mts[e])
        num = inter * qcs[e] + wvs[e]
        den = inter * jnp.sum(qhs[e] * n_sc[d, h], axis=1, keepdims=True) + jnp.sum(ws[e], axis=1, keepdims=True)
        o_refs[d][0, :, h * MLSTM_DV:(h + 1) * MLSTM_DV] = num / jnp.maximum(jnp.abs(den), jnp.exp(-mts[e]))
    for e, (d, h) in enumerate(chains):
        b, li, m_st = bs[e], lis[e], m_sts[e]
        bl = b[0:1] if d else b[L - 1:L]
        tail = bl - b + li
        m_new = jnp.maximum(bl + m_st, jnp.max(tail, axis=0, keepdims=True))
        kw = khs[e] * jnp.exp(tail - m_new)
        decay = jnp.exp(bl + m_st - m_new)
        c_sc[d, h] = decay * c_sc[d, h] + _dot_tn(kw, vhs[e])
        n_sc[d, h] = decay * n_sc[d, h] + jnp.sum(kw, axis=0, keepdims=True)
        m_sc[d, h] = jnp.broadcast_to(m_new, (1, LANES))


def _mlstm_scan(zm, zs, small_bias, lreal):
    n, tp, _ = zm.shape
    nc = tp // CHUNK
    specs = []
    for cidx in (lambda c: c, lambda c: nc - 1 - c):
        specs += [
            pl.BlockSpec((1, CHUNK, 512), lambda i, c, cidx=cidx: (i, cidx(c), C_QA // 512)),
            pl.BlockSpec((1, CHUNK, 512), lambda i, c, cidx=cidx: (i, cidx(c), C_KA // 512)),
            pl.BlockSpec((1, CHUNK, WIDTH), lambda i, c, cidx=cidx: (i, cidx(c), C_VA // WIDTH)),
            pl.BlockSpec((1, CHUNK, N_SMALL), lambda i, c, cidx=cidx: (i, cidx(c), 0)),
        ]
    return pl.pallas_call(
        functools.partial(_mlstm_body, nc=nc, lreal=lreal),
        grid=(n, nc),
        in_specs=specs + [pl.BlockSpec((1, N_SMALL), lambda i, c: (0, 0))],
        out_specs=[pl.BlockSpec((1, CHUNK, WIDTH), lambda i, c: (i, c, 0)),
                   pl.BlockSpec((1, CHUNK, WIDTH), lambda i, c: (i, nc - 1 - c, 0))],
        out_shape=[jax.ShapeDtypeStruct((n, tp, WIDTH), F32)] * 2,
        scratch_shapes=[
            pltpu.VMEM((2, MLSTM_HEADS, MLSTM_DQK, MLSTM_DV), F32),
            pltpu.VMEM((2, MLSTM_HEADS, 1, MLSTM_DQK), F32),
            pltpu.VMEM((2, MLSTM_HEADS, 1, LANES), F32),
        ],
        compiler_params=_params(dimension_semantics=("parallel", "arbitrary")),
        name="mlstm_scan",
    )(zm, zm, zm, zs, zm, zm, zm, zs, small_bias)


def _mlstm_post_body(hf_ref, hb_ref, oa_ref, ga_ref, nw_ref, o_ref):
    h = hf_ref[0] + hb_ref[0]
    for j in range(MLSTM_HEADS):
        sl = slice(j * MLSTM_DV, (j + 1) * MLSTM_DV)
        hh = h[:, sl]
        hn = hh * lax.rsqrt(jnp.mean(hh * hh, axis=1, keepdims=True) + RMS_EPS) * nw_ref[:, sl]
        o_ref[0, :, sl] = (hn * _sigmoid(oa_ref[0][:, sl]) * _silu(ga_ref[0][:, sl])).astype(o_ref.dtype)


def _mlstm_post(hf, hb, zm, norm_w):
    n, tp, _ = hf.shape
    tr = _tile(tp, 512, 16)
    return pl.pallas_call(
        _mlstm_post_body,
        grid=(n, tp // tr),
        in_specs=[
            pl.BlockSpec((1, tr, WIDTH), lambda i, r: (i, r, 0)),
            pl.BlockSpec((1, tr, WIDTH), lambda i, r: (i, r, 0)),
            pl.BlockSpec((1, tr, WIDTH), lambda i, r: (i, r, C_OA // WIDTH)),
            pl.BlockSpec((1, tr, WIDTH), lambda i, r: (i, r, C_GA // WIDTH)),
            pl.BlockSpec((1, WIDTH), lambda i, r: (0, 0)),
        ],
        out_specs=pl.BlockSpec((1, tr, WIDTH), lambda i, r: (i, r, 0)),
        out_shape=jax.ShapeDtypeStruct((n, tp, WIDTH), BF16),
        compiler_params=_params(dimension_semantics=("parallel", "parallel")),
        name="mlstm_post",
    )(hf, hb, zm, zm, norm_w.reshape(1, WIDTH))


def _cumsum_rows(tri_bf16, x):
    hi = x.astype(BF16)
    r1 = x - hi.astype(F32)
    mid = r1.astype(BF16)
    lo = (r1 - mid.astype(F32)).astype(BF16)
    dot = lambda t: jnp.dot(tri_bf16, t, preferred_element_type=F32)
    return dot(hi) + dot(mid) + dot(lo)


def _hgrn_body(q_ref, f_ref, v_ref, lb_ref, o_ref, st_sc, *, nc, lreal, rev):
    c = pl.program_id(1)
    cc = nc - 1 - c if rev else c

    @pl.when(c == 0)
    def _():
        st_sc[...] = jnp.zeros_like(st_sc)

    L = CHUNK
    sub = 8
    row = lax.broadcasted_iota(jnp.int32, (L, 1), 0)
    valid = (cc * L + row) < lreal
    r_i = lax.broadcasted_iota(jnp.int32, (L, L), 0)
    c_i = lax.broadcasted_iota(jnp.int32, (L, L), 1)
    tri = ((c_i >= r_i) if rev else (c_i <= r_i)).astype(BF16)
    ones = jnp.ones((HGRN_EXPAND, HGRN_EXPAND), BF16)

    nblk = L // HGRN_SUB
    vpb = HGRN_SUB // sub

    def group(gi, carry):
        hs = [gi * HGRN_GROUP + u for u in range(HGRN_GROUP)]
        sls = [pl.ds(pl.multiple_of(h * HGRN_EXPAND, HGRN_EXPAND), HGRN_EXPAND) for h in hs]
        qs, kks, vs, bs = [], [], [], []
        for sl in sls:
            fpre = f_ref[0, :, sl]
            lb = lb_ref[0, :, sl]
            log_f = jnp.where(valid, jnp.log(lb + (1.0 - lb) * _sigmoid(fpre)), 0.0)
            qs.append(_silu(q_ref[0, :, sl]))
            kks.append(jnp.where(valid, (1.0 - lb) * _sigmoid(-fpre), 0.0))
            vs.append(v_ref[0, :, sl])
            bs.append(_cumsum_rows(tri, log_f))
        accs = [[jnp.zeros((sub, HGRN_EXPAND), F32) for _ in range(L // sub)] for _ in hs]
        for blk in range(nblk):
            spans = []
            for s in range(blk * HGRN_SUB, (blk + 1) * HGRN_SUB):
                lo_v, hi_v = (blk * vpb, s // sub + 1) if rev else (s // sub, (blk + 1) * vpb)
                spans.append((s, lo_v, hi_v))
            reds = []
            for q, kk, b in zip(qs, kks, bs):
                pieces = []
                for s, lo_v, hi_v in spans:
                    rows = slice(lo_v * sub, hi_v * sub)
                    seen = (row[rows] <= s) if rev else (row[rows] >= s)
                    pieces.append((q[rows] * kk[s:s + 1]) * jnp.exp(jnp.where(seen, b[rows] - b[s:s + 1], NEG)))
                reds.append(jnp.dot(jnp.concatenate(pieces, axis=0).astype(BF16), ones, preferred_element_type=F32))
            for acc, red, v in zip(accs, reds, vs):
                r0 = 0
                for s, lo_v, hi_v in spans:
                    for jv in range(lo_v, hi_v):
                        acc[jv] = acc[jv] + red[r0:r0 + sub] * v[s:s + 1]
                        r0 += sub
        for blk in range(nblk):
            src = slice((blk + 1) * HGRN_SUB, L) if rev else slice(0, blk * HGRN_SUB)
            if src.start == src.stop:
                continue
            tgt = slice(blk * HGRN_SUB, (blk + 1) * HGRN_SUB)
            edge = (blk + 1) * HGRN_SUB if rev else blk * HGRN_SUB - 1
            atts = []
            for q, kk, b in zip(qs, kks, bs):
                beta = b[edge:edge + 1]
                qt = (q[tgt] * jnp.exp(b[tgt] - beta)).astype(BF16)
                kt = (kk[src] * jnp.exp(beta - b[src])).astype(BF16)
                atts.append(_dot_nt(qt, kt).astype(BF16))
            for acc, att, v in zip(accs, atts, vs):
                o_blk = jnp.dot(att, v[src].astype(BF16), preferred_element_type=F32)
                for jv in range(vpb):
                    acc[blk * vpb + jv] = acc[blk * vpb + jv] + o_blk[jv * sub:(jv + 1) * sub]
        sts = [st_sc[h] for h in hs]
        inters = [_dot_nt((q * jnp.exp(b)).astype(BF16), st.astype(BF16)) for q, b, st in zip(qs, bs, sts)]
        for sl, acc, inter in zip(sls, accs, inters):
            o_ref[0, :, sl] = jnp.concatenate(acc, axis=0) + inter
        for h, st, v, kk, b in zip(hs, sts, vs, kks, bs):
            bl = b[0:1] if rev else b[L - 1:L]
            st_sc[h] = st * jnp.exp(bl) + _dot_tn(v, kk * jnp.exp(bl - b))
        return carry

    lax.fori_loop(0, HGRN_HEADS // HGRN_GROUP, group, 0)


def _hgrn_scan(zm, lb, lreal, rev):
    n, tp, _ = zm.shape
    nc = tp // CHUNK
    cidx = (lambda c: nc - 1 - c) if rev else (lambda c: c)
    return pl.pallas_call(
        functools.partial(_hgrn_body, nc=nc, lreal=lreal, rev=rev),
        grid=(n, nc),
        in_specs=[
            pl.BlockSpec((1, CHUNK, WIDTH), lambda i, c: (i, cidx(c), C_QB // WIDTH)),
            pl.BlockSpec((1, CHUNK, WIDTH), lambda i, c: (i, cidx(c), C_FB // WIDTH + int(rev))),
            pl.BlockSpec((1, CHUNK, WIDTH), lambda i, c: (i, cidx(c), C_IB // WIDTH)),
            pl.BlockSpec((1, 1, WIDTH), lambda i, c: (int(rev), 0, 0)),
        ],
        out_specs=pl.BlockSpec((1, CHUNK, WIDTH), lambda i, c: (i, cidx(c), 0)),
        out_shape=jax.ShapeDtypeStruct((n, tp, WIDTH), F32),
        scratch_shapes=[pltpu.VMEM((HGRN_HEADS, HGRN_EXPAND, HGRN_EXPAND), F32)],
        compiler_params=_params(dimension_semantics=("parallel", "arbitrary")),
        name="hgrn_scan_bwd" if rev else "hgrn_scan_fwd",
    )(zm, zm, zm, lb)


def _hgrn_post_body(of_ref, ob_ref, gb_ref, nw_ref, y_ref):
    o = of_ref[0] + ob_ref[0]
    for j in range(HGRN_HEADS):
        sl = slice(j * HGRN_EXPAND, (j + 1) * HGRN_EXPAND)
        oh = o[:, sl]
        on = oh * lax.rsqrt(jnp.mean(oh * oh, axis=1, keepdims=True) + RMS_EPS) * nw_ref[...]
        y_ref[0, :, sl] = (on * _silu(gb_ref[0][:, sl])).astype(y_ref.dtype)


def _hgrn_post(of, ob, zm, norm_w):
    n, tp, _ = of.shape
    tr = _tile(tp, 512, 16)
    spec = pl.BlockSpec((1, tr, WIDTH), lambda i, r: (i, r, 0))
    return pl.pallas_call(
        _hgrn_post_body,
        grid=(n, tp // tr),
        in_specs=[spec, spec,
                  pl.BlockSpec((1, tr, WIDTH), lambda i, r: (i, r, C_GB // WIDTH)),
                  pl.BlockSpec((1, HGRN_EXPAND), lambda i, r: (0, 0))],
        out_specs=spec,
        out_shape=jax.ShapeDtypeStruct((n, tp, WIDTH), BF16),
        compiler_params=_params(dimension_semantics=("parallel", "parallel")),
        name="hgrn_post",
    )(of, ob, zm, norm_w.reshape(1, HGRN_EXPAND))


def _shift_body(x_ref, mu_ref, o_ref, *, lreal):
    x = x_ref[0]
    tp = x.shape[0]
    row = lax.broadcasted_iota(jnp.int32, (tp, 1), 0)
    prev = jnp.where(row == 0, 0.0, pltpu.roll(x, 1, 0))
    nxt = jnp.where(row + 1 >= lreal, 0.0, pltpu.roll(x, tp - 1, 0))
    o_ref[0] = x + mu_ref[0:1, :] * (prev - x) + mu_ref[1:2, :] * (nxt - x)


def _token_shift(z, col0, ncols, mu, lreal):
    n, tp, _ = z.shape
    tc = LANES
    return pl.pallas_call(
        functools.partial(_shift_body, lreal=lreal),
        grid=(n, ncols // tc),
        in_specs=[
            pl.BlockSpec((1, tp, tc), lambda i, j: (i, 0, col0 // tc + j)),
            pl.BlockSpec((2, tc), lambda i, j: (0, j)),
        ],
        out_specs=pl.BlockSpec((1, tp, tc), lambda i, j: (i, 0, j)),
        out_shape=jax.ShapeDtypeStruct((n, tp, ncols), F32),
        compiler_params=_params(dimension_semantics=("parallel", "parallel")),
        name="token_shift",
    )(z, mu)


def _seg_ones():
    r = lax.broadcasted_iota(jnp.int32, (LANES, LANES), 0) // RWKV_HEAD
    c = lax.broadcasted_iota(jnp.int32, (LANES, LANES), 1) // RWKV_HEAD
    return (r == c).astype(BF16)


def _head_sum(x, seg):
    hi = x.astype(BF16)
    r1 = x - hi.astype(F32)
    mid = r1.astype(BF16)
    lo = (r1 - mid.astype(F32)).astype(BF16)
    parts = []
    for j in range(WIDTH // LANES):
        sl = slice(j * LANES, (j + 1) * LANES)
        parts.append(jnp.dot(hi[:, sl], seg, preferred_element_type=F32)
                     + jnp.dot(mid[:, sl], seg, preferred_element_type=F32)
                     + jnp.dot(lo[:, sl], seg, preferred_element_type=F32))
    return jnp.concatenate(parts, axis=1)


def _rwkv_pre_body(f_ref, fs_ref, w2_ref, a2_ref, w0_ref, a0_ref, kk_ref, ka_ref, rk_ref,
                   r_o, v_o, kk_o, wf_o, wb_o, nf_o, nb_o, kf_o, kb_o, bonus_o, *, lreal):
    f = f_ref[0]
    r = f[:, 0:WIDTH]
    k = f[:, WIDTH:2 * WIDTH]
    v = f[:, 2 * WIDTH:3 * WIDTH]
    fs = fs_ref[0]
    wl = fs[:, 0:LANES]
    al = fs[:, LANES:2 * LANES]
    seg = _seg_ones()
    w = w0_ref[...] + jnp.dot(jnp.tanh(wl).astype(BF16), w2_ref[...], preferred_element_type=F32)
    wdec = jnp.exp(-jnp.exp(-_softplus(-w) - 0.5))
    a = _sigmoid(a0_ref[...] + jnp.dot(al.astype(BF16), a2_ref[...], preferred_element_type=F32))
    kk = k * kk_ref[...]
    kk = kk * jnp.minimum(lax.rsqrt(_head_sum(kk * kk, seg)), 1e12)
    a_f, a_b = a[:, 0:WIDTH], a[:, WIDTH:2 * WIDTH]
    kd_f = k * (1.0 + (a_f - 1.0) * ka_ref[...])
    kd_b = k * (1.0 + (a_b - 1.0) * ka_ref[...])
    tr = f.shape[0]
    valid = (pl.program_id(1) * tr + lax.broadcasted_iota(jnp.int32, (tr, 1), 0)) < lreal
    keep = lambda x: jnp.where(valid, x, 0.0)

    def put(o_ref, x):
        npair = WIDTH // LANES
        for p in range(npair):
            o_ref[0, pl.ds(p, tr, stride=npair), :] = x[:, p * LANES:(p + 1) * LANES]

    put(r_o, keep(r))
    put(v_o, keep(v))
    put(kk_o, keep(kk))
    put(wf_o, jnp.where(valid, wdec[:, 0:WIDTH], 1.0))
    put(wb_o, jnp.where(valid, wdec[:, WIDTH:2 * WIDTH], 1.0))
    put(nf_o, keep(-(kk * a_f)))
    put(nb_o, keep(-(kk * a_b)))
    put(kf_o, keep(kd_f))
    put(kb_o, keep(kd_b))
    bonus_o[0] = _head_sum(r * (kd_f + kd_b) * rk_ref[...], seg) * v


def _rwkv_pre(fsh, fsm, w2cat, a2cat, w0, a0, k_k, k_a, r_k, lreal):
    n, tp, _ = fsh.shape
    tr = _tile(tp, 256, 8)
    row = lambda width: pl.BlockSpec((1, width), lambda i, r: (0, 0))
    out_spec = pl.BlockSpec((1, tr, WIDTH), lambda i, r: (i, r, 0))
    npair = WIDTH // LANES
    outs = pl.pallas_call(
        functools.partial(_rwkv_pre_body, lreal=lreal),
        grid=(n, tp // tr),
        in_specs=[
            pl.BlockSpec((1, tr, 3 * WIDTH), lambda i, r: (i, r, 0)),
            pl.BlockSpec((1, tr, 2 * LANES), lambda i, r: (i, r, 0)),
            pl.BlockSpec((LANES, 2 * WIDTH), lambda i, r: (0, 0)),
            pl.BlockSpec((LANES, 2 * WIDTH), lambda i, r: (0, 0)),
            row(2 * WIDTH), row(2 * WIDTH), row(WIDTH), row(WIDTH), row(WIDTH),
        ],
        out_specs=[pl.BlockSpec((1, tr * npair, LANES), lambda i, r: (i, r, 0))] * 9 + [out_spec],
        out_shape=[jax.ShapeDtypeStruct((n, tp * npair, LANES), F32)] * 9
        + [jax.ShapeDtypeStruct((n, tp, WIDTH), F32)],
        compiler_params=_params(dimension_semantics=("parallel", "parallel")),
        name="rwkv_pre",
    )(fsh, fsm, w2cat, a2cat, w0, a0, k_k, k_a, r_k)
    return [o.reshape(n, tp, npair, LANES) for o in outs[:9]] + [outs[9]]


RWKV_PAIRS = WIDTH // LANES
RWKV_TB = 16
RWKV_NV1 = 8


def _sum_interleaved(terms, nacc):
    accs = []
    for i, t in enumerate(terms):
        if i < nacc:
            accs.append(t)
        else:
            accs[i % nacc] = accs[i % nacc] + t
    while len(accs) > 1:
        accs = [accs[i] + accs[i + 1] for i in range(0, len(accs), 2)]
    return accs[0]


def _rwkv_chain_step(s_sc, t_sc, load_v, g, nacc, nsplit=1):
    base = g * RWKV_HEAD
    row = lambda i, k: t_sc[i, base + k:base + k + 1, :]
    nv = s_sc.shape[2]
    part = nv // nsplit
    outs = []
    for h in range(nsplit):
        rs = slice(h * part, (h + 1) * part)
        sa = _sum_interleaved((s_sc[g, k, rs, :] * row(1, k) for k in range(RWKV_HEAD)), nacc)
        v_h = load_v()[rs]

        def update(k):
            s = s_sc[g, k, rs, :] * row(0, k) + sa * row(2, k) + v_h * row(3, k)
            s_sc[g, k, rs, :] = s
            return s * row(4, k)

        outs.append(_sum_interleaved((update(k) for k in range(RWKV_HEAD)), nacc))
    return outs[0] if nsplit == 1 else jnp.concatenate(outs, axis=0)


def _rwkv_pipelined_steps(build, compute, t_a, t_b):
    build(0, t_a)

    def pair(j2, carry):
        j = 2 * j2
        build(j + 1, t_b)
        compute(j, t_a)
        build(jnp.minimum(j + 2, RWKV_TB - 1), t_a)
        compute(j + 1, t_b)
        return carry

    lax.fori_loop(0, RWKV_TB // 2, pair, 0)


def _rwkv_scan8_body(wf, wb, kkf, kkb, nf, nb, kdf, kdb, rf, rb, vf, vb, yf_ref, yb_ref, s_sc, t_a, t_b):
    @pl.when(pl.program_id(0) == 0)
    def _():
        s_sc[...] = jnp.zeros_like(s_sc)

    nseq = wf.shape[0]
    operands = ((wf, wb), (kkf, kkb), (nf, nb), (kdf, kdb), (rf, rb), (vf, vb))

    def build(j, t_sc):
        jb = RWKV_TB - 1 - j
        for i, (xf_ref, xb_ref) in enumerate(operands):
            rows = [xf_ref[s, j] for s in range(nseq)] + [xb_ref[s, jb] for s in range(nseq)]
            t_sc[i] = jnp.concatenate(rows, axis=0).T

    def compute(j, t_sc):
        jb = RWKV_TB - 1 - j
        ys = [_rwkv_chain_step(s_sc, t_sc, lambda g=g: t_sc[5, g * RWKV_HEAD:(g + 1) * RWKV_HEAD, :], g, 4, 2)
              for g in range(2)]
        yt = jnp.concatenate(ys, axis=0).T
        for s in range(nseq):
            yf_ref[s, j] = yt[s * RWKV_PAIRS:(s + 1) * RWKV_PAIRS]
            yb_ref[s, jb] = yt[(nseq + s) * RWKV_PAIRS:(nseq + s + 1) * RWKV_PAIRS]

    _rwkv_pipelined_steps(build, compute, t_a, t_b)


def _rwkv_scan1_body(wf, wb, kkf, kkb, nf, nb, kdf, kdb, rf, rb, vf, vb, yf_ref, yb_ref, s_sc, t_a, t_b):
    @pl.when(pl.program_id(0) == 0)
    def _():
        s_sc[...] = jnp.zeros_like(s_sc)

    nv = RWKV_NV1
    noct = RWKV_HEAD // nv
    operands = ((wf, wb), (kkf, kkb), (nf, nb), (kdf, kdb), (rf, rb), (vf, vb))

    def build(j, t_sc):
        jb = RWKV_TB - 1 - j
        for i, (xf_ref, xb_ref) in enumerate(operands):
            rows = [jnp.broadcast_to(xf_ref[0, j, p:p + 1, :], (8, LANES)) for p in range(RWKV_PAIRS)]
            rows += [jnp.broadcast_to(xb_ref[0, jb, p:p + 1, :], (8, LANES)) for p in range(RWKV_PAIRS)]
            t_sc[i] = jnp.concatenate(rows, axis=0).T

    def compute(j, t_sc):
        jb = RWKV_TB - 1 - j
        octet = lax.broadcasted_iota(jnp.int32, (nv, LANES), 1) % noct
        ys = []
        for g in range(2):
            v_g = jnp.zeros((nv, LANES), F32)
            for q in range(noct):
                r0 = g * RWKV_HEAD + q * nv
                v_g = jnp.where(octet == q, t_sc[5, r0:r0 + nv, :], v_g)
            ys.append(_rwkv_chain_step(s_sc, t_sc, lambda v_g=v_g: v_g, g, 8))
        zrows = [jnp.where(octet == q, ys[g], 0.0) for g in range(2) for q in range(noct)]
        yt = jnp.concatenate(zrows, axis=0).T
        for dp in range(2 * RWKV_PAIRS):
            rowv = jnp.sum(yt[dp * 8:(dp + 1) * 8], axis=0, keepdims=True)
            p = dp % RWKV_PAIRS
            if dp < RWKV_PAIRS:
                yf_ref[0, j, p:p + 1, :] = rowv
            else:
                yb_ref[0, jb, p:p + 1, :] = rowv

    _rwkv_pipelined_steps(build, compute, t_a, t_b)


def _rwkv_scan(wf, wb, kk, nf, nb, kdf, kdb, r, v):
    n, tp = r.shape[:2]
    nblk = tp // RWKV_TB
    fspec = pl.BlockSpec((n, RWKV_TB, RWKV_PAIRS, LANES), lambda i: (0, i, 0, 0))
    bspec = pl.BlockSpec((n, RWKV_TB, RWKV_PAIRS, LANES), lambda i: (0, nblk - 1 - i, 0, 0))
    if n == 8:
        body, nv = _rwkv_scan8_body, RWKV_HEAD
    else:
        assert n == 1
        body, nv = _rwkv_scan1_body, RWKV_NV1
    return pl.pallas_call(
        body,
        grid=(nblk,),
        in_specs=[fspec, bspec] * 6,
        out_specs=[fspec, bspec],
        out_shape=[jax.ShapeDtypeStruct((n, tp, RWKV_PAIRS, LANES), F32)] * 2,
        scratch_shapes=[pltpu.VMEM((2, RWKV_HEAD, nv, LANES), F32),
                        pltpu.VMEM((6, LANES, LANES), F32), pltpu.VMEM((6, LANES, LANES), F32)],
        compiler_params=_params(dimension_semantics=("arbitrary",)),
        name="rwkv_scan",
    )(wf, wb, kk, kk, nf, nb, kdf, kdb, r, r, v, v)


def _rwkv_post_body(yf_ref, yb_ref, bonus_ref, gc_ref, lnw_ref, lnb_ref, o_ref):
    seg = _seg_ones()
    y = jnp.concatenate([yf_ref[0, :, p, :] + yb_ref[0, :, p, :] for p in range(RWKV_PAIRS)], axis=1)
    mean = _head_sum(y, seg) * (1.0 / RWKV_HEAD)
    yc = y - mean
    var = _head_sum(yc * yc, seg) * (1.0 / RWKV_HEAD)
    yn = yc * lax.rsqrt(var + RWKV_GN_EPS) * lnw_ref[...] + lnb_ref[...]
    o_ref[0] = ((yn + bonus_ref[0]) * _silu(gc_ref[0])).astype(o_ref.dtype)


def _rwkv_post(yf, yb, bonus, zm, ln_w, ln_b):
    n, tp = yf.shape[:2]
    tr = _tile(tp, 512, 16)
    spec = pl.BlockSpec((1, tr, WIDTH), lambda i, r: (i, r, 0))
    yspec = pl.BlockSpec((1, tr, RWKV_PAIRS, LANES), lambda i, r: (i, r, 0, 0))
    row = pl.BlockSpec((1, WIDTH), lambda i, r: (0, 0))
    return pl.pallas_call(
        _rwkv_post_body,
        grid=(n, tp // tr),
        in_specs=[yspec, yspec, spec, pl.BlockSpec((1, tr, WIDTH), lambda i, r: (i, r, C_GC // WIDTH)), row, row],
        out_specs=spec,
        out_shape=jax.ShapeDtypeStruct((n, tp, WIDTH), BF16),
        compiler_params=_params(dimension_semantics=("parallel", "parallel")),
        name="rwkv_post",
    )(yf, yb, bonus, zm, ln_w.reshape(1, WIDTH), ln_b.reshape(1, WIDTH))


def _rwkv_branch(zm, zs, p, lreal):
    fsh = _token_shift(zm, C_RC, 3 * WIDTH, p['mu_main'], lreal)
    fsm = _token_shift(zs, 0, 2 * LANES, p['mu_small'], lreal)
    r, v, kk, wf, wb, nf, nb, kf, kb, bonus = _rwkv_pre(
        fsh, fsm, p['w2cat'], p['a2cat'], p['w0'], p['a0'], p['k_k'], p['k_a'], p['r_k'], lreal)
    yf, yb = _rwkv_scan(wf, wb, kk, nf, nb, kf, kb, r, v)
    return _rwkv_post(yf, yb, bonus, zm, p['ln_w'], p['ln_b'])


def _s5_core_body(u_ref, m_ref, n_ref, q_ref, pw_ref, cidx_ref, y_ref, *, nsteps, nvalid, nblocks):
    cidx = cidx_ref[...]
    u = jnp.where(cidx < nvalid, u_ref[0], 0.0).astype(BF16)
    rows = u.shape[0]
    y = None
    for d in range(2):
        yd = jnp.dot(u, m_ref[d, 0], preferred_element_type=F32)
        x = jnp.dot(u, n_ref[d, 0], preferred_element_type=F32)
        for j in range(nsteps):
            s = 1 << j
            if d == 0:
                xs = jnp.where(cidx >= s, pltpu.roll(x, s, 0), 0.0)
            else:
                xs = jnp.where(cidx + s < nblocks, pltpu.roll(x, rows - s, 0), 0.0)
            x = (x + xs * pw_ref[d, 0, 2 * j:2 * j + 1, :]
                 + pltpu.roll(xs, S5_STATE, 1) * pw_ref[d, 0, 2 * j + 1:2 * j + 2, :])
        if d == 0:
            xin = jnp.where(cidx >= 1, pltpu.roll(x, 1, 0), 0.0)
        else:
            xin = jnp.where(cidx + 1 < nblocks, pltpu.roll(x, rows - 1, 0), 0.0)
        yd = yd + jnp.dot(xin.astype(BF16), q_ref[d, 0], preferred_element_type=F32)
        y = yd if y is None else y + yd
    y_ref[0] = y


def _s5_core(u, mats, cidx, nsteps, nvalid, nblocks):
    g, rows, _ = u.shape
    mm, nn, qq, pw = mats
    return pl.pallas_call(
        functools.partial(_s5_core_body, nsteps=nsteps, nvalid=nvalid, nblocks=nblocks),
        grid=(g,),
        in_specs=[
            pl.BlockSpec((1, rows, S5_ROW), lambda j: (j, 0, 0)),
            pl.BlockSpec((2, 1, S5_ROW, S5_ROW), lambda j: (0, j, 0, 0)),
            pl.BlockSpec((2, 1, S5_ROW, 2 * S5_STATE), lambda j: (0, j, 0, 0)),
            pl.BlockSpec((2, 1, 2 * S5_STATE, S5_ROW), lambda j: (0, j, 0, 0)),
            pl.BlockSpec((2, 1, pw.shape[2], 2 * S5_STATE), lambda j: (0, j, 0, 0)),
            pl.BlockSpec((rows, 1), lambda j: (0, 0)),
        ],
        out_specs=pl.BlockSpec((1, rows, S5_ROW), lambda j: (j, 0, 0)),
        out_shape=jax.ShapeDtypeStruct((g, rows, S5_ROW), F32),
        compiler_params=_params(dimension_semantics=("parallel",)),
        name="s5_core",
    )(u, mm, nn, qq, pw, cidx)


def _s5_post_body(y_ref, ud_ref, gd_ref, d_ref, w_ref, b_ref, o_ref):
    y = y_ref[0] + d_ref[...] * ud_ref[0]
    g = 0.5 * y * (1.0 + jnp.tanh(0.7978845608028654 * (y + 0.044715 * (y * y * y))))
    glu = jnp.dot(g.astype(BF16), w_ref[...], preferred_element_type=F32) + b_ref[...]
    o_ref[0] = (g * _sigmoid(glu) * _silu(gd_ref[0])).astype(o_ref.dtype)


def _s5_post(ys, zm, d, glu_w, glu_b):
    n, tp, _ = ys.shape
    tr = _tile(tp, 512, 16)
    spec = pl.BlockSpec((1, tr, WIDTH), lambda i, r: (i, r, 0))
    row = pl.BlockSpec((1, WIDTH), lambda i, r: (0, 0))
    return pl.pallas_call(
        _s5_post_body,
        grid=(n, tp // tr),
        in_specs=[spec,
                  pl.BlockSpec((1, tr, WIDTH), lambda i, r: (i, r, C_UD // WIDTH)),
                  pl.BlockSpec((1, tr, WIDTH), lambda i, r: (i, r, C_GD // WIDTH)),
                  row, pl.BlockSpec((WIDTH, WIDTH), lambda i, r: (0, 0)), row],
        out_specs=spec,
        out_shape=jax.ShapeDtypeStruct((n, tp, WIDTH), BF16),
        compiler_params=_params(dimension_semantics=("parallel", "parallel")),
        name="s5_post",
    )(ys, zm, zm, d.reshape(1, WIDTH), glu_w, glu_b.reshape(1, WIDTH))


def _s5_mats(a_re, a_im, log_dt, b_re, b_im, c_re, c_im, max_blocks):
    dt = jnp.exp(log_dt)[..., None]
    mag = jnp.exp(a_re * dt)
    ang = a_im * dt
    ab_re, ab_im = mag * jnp.cos(ang), mag * jnp.sin(ang)
    den = a_re * a_re + a_im * a_im
    xr, yi = ab_re - 1.0, ab_im
    coef_re = (xr * a_re + yi * a_im) / den
    coef_im = (yi * a_re - xr * a_im) / den
    bb_re = coef_re[..., None] * b_re - coef_im[..., None] * b_im
    bb_im = coef_re[..., None] * b_im + coef_im[..., None] * b_re
    pr, pi = [jnp.ones_like(ab_re)], [jnp.zeros_like(ab_re)]
    for _ in range(S5_BLOCK):
        pr, pi = pr + [pr[-1] * ab_re - pi[-1] * ab_im], pi + [pr[-1] * ab_im + pi[-1] * ab_re]
    pw_re, pw_im = jnp.stack(pr), jnp.stack(pi)
    t_re = pw_re[..., None] * bb_re - pw_im[..., None] * bb_im
    t_im = pw_re[..., None] * bb_im + pw_im[..., None] * bb_re
    kj = (jnp.einsum('dgcp,jdgpe->jdgce', c_re, t_re, precision=HI)
          - jnp.einsum('dgcp,jdgpe->jdgce', c_im, t_im, precision=HI))
    s_i = jnp.arange(S5_BLOCK)[:, None]
    i_i = jnp.arange(S5_BLOCK)[None, :]
    rev = jnp.arange(S5_BLOCK - 1, -1, -1)

    def tables(d):
        dist = (s_i - i_i) if d else (i_i - s_i)
        m6 = jnp.where((dist >= 0)[:, :, None, None, None], kj[jnp.clip(dist, 0, S5_BLOCK), d], 0.0)
        mm_d = jnp.transpose(m6, (2, 0, 4, 1, 3)).reshape(S5_GROUPS, S5_ROW, S5_ROW)
        order = jnp.arange(S5_BLOCK) if d else rev
        n_re = jnp.transpose(t_re[order, d], (1, 0, 3, 2)).reshape(S5_GROUPS, S5_ROW, S5_STATE)
        n_im = jnp.transpose(t_im[order, d], (1, 0, 3, 2)).reshape(S5_GROUPS, S5_ROW, S5_STATE)
        steps_in = (rev + 1) if d else (jnp.arange(S5_BLOCK) + 1)
        pr_i, pi_i = pw_re[steps_in, d][:, :, None, :], pw_im[steps_in, d][:, :, None, :]
        ca_re = c_re[d][None] * pr_i - c_im[d][None] * pi_i
        ca_im = c_re[d][None] * pi_i + c_im[d][None] * pr_i
        q_re = jnp.transpose(ca_re, (1, 3, 0, 2)).reshape(S5_GROUPS, S5_STATE, S5_ROW)
        q_im = jnp.transpose(-ca_im, (1, 3, 0, 2)).reshape(S5_GROUPS, S5_STATE, S5_ROW)
        return mm_d, jnp.concatenate([n_re, n_im], axis=-1), jnp.concatenate([q_re, q_im], axis=1)

    mm, nn, qq = (jnp.stack(t).astype(BF16) for t in zip(tables(0), tables(1)))
    sr, si = pw_re[S5_BLOCK], pw_im[S5_BLOCK]
    rows = []
    steps = 0
    while (1 << steps) < max_blocks:
        rows += [jnp.concatenate([sr, sr], axis=-1), jnp.concatenate([-si, si], axis=-1)]
        sr, si = sr * sr - si * si, 2.0 * sr * si
        steps += 1
    while len(rows) % 8:
        rows.append(jnp.zeros_like(rows[0]))
    pw = jnp.stack(rows, axis=2)
    return (mm, nn, qq, pw), steps


def _s5_branch(zm, p, lreal):
    n, tp, _ = zm.shape
    assert lreal % S5_BLOCK == 0 and tp % S5_BLOCK == 0
    nvalid = lreal // S5_BLOCK
    nb = tp // S5_BLOCK
    rows = n * nb
    rows_p = -(-rows // 8) * 8
    u = zm[:, :, C_UD:C_UD + WIDTH].reshape(n, nb, S5_BLOCK, S5_GROUPS, S5_GROUP)
    u = jnp.transpose(u, (3, 0, 1, 2, 4)).reshape(S5_GROUPS, rows, S5_ROW)
    u = jnp.pad(u, ((0, 0), (0, rows_p - rows), (0, 0)))
    cidx = jnp.pad(jnp.tile(jnp.arange(nb, dtype=jnp.int32), n), (0, rows_p - rows),
                   constant_values=nb).reshape(rows_p, 1)
    mats, nsteps = p['s5_mats'](nb)
    y = _s5_core(u, mats, cidx, nsteps, nvalid, nb)[:, :rows]
    ys = jnp.transpose(y.reshape(S5_GROUPS, n, nb, S5_BLOCK, S5_GROUP), (1, 2, 3, 0, 4)).reshape(n, tp, WIDTH)
    return _s5_post(ys, zm, p['s5_d'], p['glu_w'], p['glu_b'])


def _merge_body(h_ref, ya_ref, yb_ref, yc_ref, yd_ref, wg_ref, bg_ref, wb_ref, o_ref):
    h = h_ref[...]
    acc = None
    for b, y_ref in enumerate((ya_ref, yb_ref, yc_ref, yd_ref)):
        gate = _sigmoid(jnp.dot(h, wg_ref[b], preferred_element_type=F32) + bg_ref[b])
        term = gate * jnp.dot(y_ref[...], wb_ref[b], preferred_element_type=F32)
        acc = term if acc is None else acc + term
    o_ref[...] = acc.astype(o_ref.dtype)


def _merge(h, ys, wg, bg, wb):
    m, d = h.shape
    tm = _tile(m, 1024, 16)
    tn = 256
    yspec = pl.BlockSpec((tm, WIDTH), lambda i, j: (i, 0))
    return pl.pallas_call(
        _merge_body,
        grid=(m // tm, d // tn),
        in_specs=[pl.BlockSpec((tm, d), lambda i, j: (i, 0)), yspec, yspec, yspec, yspec,
                  pl.BlockSpec((4, d, tn), lambda i, j: (0, 0, j)),
                  pl.BlockSpec((4, 1, tn), lambda i, j: (0, 0, j)),
                  pl.BlockSpec((4, WIDTH, tn), lambda i, j: (0, 0, j))],
        out_specs=pl.BlockSpec((tm, tn), lambda i, j: (i, j)),
        out_shape=jax.ShapeDtypeStruct((m, d), BF16),
        compiler_params=_params(dimension_semantics=("parallel", "parallel")),
        name="merge",
    )(h, *ys, wg, bg, wb)


def _layer(xp, p, lreal):
    n, tp, d = xp.shape
    m = n * tp
    x2 = xp.reshape(m, d)
    h = _rmsnorm(x2, p['norm_w'], BF16)
    zm = _mm(h, p['w_main'], name="proj_main").reshape(n, tp, N_MAIN)
    zs = _mm(h, p['w_small'], name="proj_small").reshape(n, tp, N_SMALL)
    ya = _mlstm_post(*_mlstm_scan(zm, zs, p['small_bias'], lreal), zm, p['mlstm_norm_w'])
    yb = _hgrn_post(_hgrn_scan(zm, p['hgrn_lb'], lreal, False), _hgrn_scan(zm, p['hgrn_lb'], lreal, True), zm,
                    p['hgrn_norm_w'])
    yc = _rwkv_branch(zm, zs, p, lreal)
    yd = _s5_branch(zm, p, lreal)
    ys = [y.reshape(m, WIDTH) for y in (ya, yb, yc, yd)]
    merged = _merge(h, ys, p['w_gate'], p['b_gate'], p['w_branch'])
    return _mm(merged, p['w_out'], res=x2, name="proj_out").reshape(n, tp, d)


def _trunk(x, meta, layers, final_norm_w):
    n, t, d = x.shape
    lreal = t + N_META
    tp = lreal + TAIL_PAD
    xp = jnp.concatenate([jnp.broadcast_to(meta[None], (n, N_META, d)), x, jnp.zeros((n, TAIL_PAD, d), x.dtype)], axis=1)
    for p in layers:
        xp = _layer(xp, p, lreal)
    y = _rmsnorm(xp.reshape(n * tp, d), final_norm_w, F32).reshape(n, tp, d)
    return y[:, N_META:lreal]


def _layer_params(l, a):
    w_in = a['w_in'][l]
    offs = {}
    acc = 0
    sizes = (512, 512, WIDTH, WIDTH, 8, 8, WIDTH, WIDTH, 2 * WIDTH, WIDTH, WIDTH, WIDTH, WIDTH, WIDTH,
             2 * RWKV_LORA, 2 * RWKV_LORA, WIDTH, WIDTH, WIDTH)
    names = ('qa', 'ka', 'va', 'oa', 'iga', 'fga', 'ga', 'qb', 'fb', 'ib', 'gb', 'rc', 'kc', 'vc', 'wlc', 'alc',
             'gc', 'ud', 'gd')
    for nm, sz in zip(names, sizes):
        offs[nm] = (acc, acc + sz)
        acc += sz
    col = lambda nm: w_in[:, offs[nm][0]:offs[nm][1]]
    main_order = ('qa', 'ka', 'va', 'oa', 'ga', 'qb', 'fb', 'ib', 'gb', 'rc', 'kc', 'vc', 'gc', 'ud', 'gd')
    w_main = jnp.concatenate([col(nm) for nm in main_order], axis=1).astype(BF16)
    w_small = jnp.concatenate([col('wlc'), col('alc'), col('iga'), col('fga'),
                               jnp.zeros((D_MODEL, LANES - 16), F32)], axis=1).astype(BF16)
    small_bias = jnp.concatenate([jnp.zeros((G_OFF,), F32), a['mlstm_ig_b'][l].reshape(-1),
                                  a['mlstm_fg_b'][l].reshape(-1), jnp.zeros((LANES - 16,), F32)]).reshape(1, N_SMALL)
    lbw = jax.nn.softmax(a['hgrn_lower_bounds'].astype(F32), axis=1)
    hgrn_lb = (jnp.cumsum(lbw, axis=1) - lbw[:, :1])[:, l].reshape(2, 1, WIDTH)
    mu = a['rwkv_shift_mu'][l]
    zero = jnp.zeros((RWKV_LORA, WIDTH), F32)
    blockdiag = lambda w2: jnp.concatenate([jnp.concatenate([w2[0], zero], axis=1),
                                            jnp.concatenate([zero, w2[1]], axis=1)], axis=0)
    s5_args = tuple(a[k][l].astype(F32) for k in ('s5_a_re', 's5_a_im', 's5_log_dt', 's5_b_re', 's5_b_im',
                                                  's5_c_re', 's5_c_im'))
    return {
        'norm_w': a['norm_w'][l], 'w_main': w_main, 'w_small': w_small, 'small_bias': small_bias,
        'mlstm_norm_w': a['mlstm_norm_w'][l], 'hgrn_lb': hgrn_lb, 'hgrn_norm_w': a['hgrn_norm_w'][l],
        'mu_main': mu[:, :3 * WIDTH], 'mu_small': mu[:, 3 * WIDTH:],
        'w2cat': blockdiag(a['rwkv_w2'][l]).astype(BF16), 'a2cat': blockdiag(a['rwkv_a2'][l]).astype(BF16),
        'w0': a['rwkv_w0'][l].reshape(1, 2 * WIDTH), 'a0': a['rwkv_a0'][l].reshape(1, 2 * WIDTH),
        'k_k': a['rwkv_k_k'][l].reshape(1, WIDTH), 'k_a': a['rwkv_k_a'][l].reshape(1, WIDTH),
        'r_k': a['rwkv_r_k'][l].reshape(1, WIDTH),
        'ln_w': a['rwkv_ln_w'][l], 'ln_b': a['rwkv_ln_b'][l],
        's5_mats': functools.partial(_s5_mats, *s5_args),
        's5_d': a['s5_d'][l], 'glu_w': a['s5_glu_w'][l].astype(BF16), 'glu_b': a['s5_glu_b'][l],
        'w_gate': a['w_gate'][l].astype(BF16), 'b_gate': a['b_gate'][l].reshape(4, 1, D_MODEL),
        'w_branch': a['w_branch'][l].astype(BF16), 'w_out': a['w_out'][l].astype(BF16),
    }


def kernel(x_prompt, x_sample, meta_tokens, norm_w, w_in, mlstm_ig_b, mlstm_fg_b, mlstm_norm_w, hgrn_lower_bounds, hgrn_norm_w, rwkv_shift_mu, rwkv_w0, rwkv_w2, rwkv_a0, rwkv_a2, rwkv_k_k, rwkv_k_a, rwkv_r_k, rwkv_ln_w, rwkv_ln_b, s5_a_re, s5_a_im, s5_log_dt, s5_b_re, s5_b_im, s5_c_re, s5_c_im, s5_d, s5_glu_w, s5_glu_b, w_branch, w_gate, b_gate, w_out, final_norm_w):
    a = dict(norm_w=norm_w, w_in=w_in, mlstm_ig_b=mlstm_ig_b, mlstm_fg_b=mlstm_fg_b, mlstm_norm_w=mlstm_norm_w,
             hgrn_lower_bounds=hgrn_lower_bounds, hgrn_norm_w=hgrn_norm_w, rwkv_shift_mu=rwkv_shift_mu,
             rwkv_w0=rwkv_w0, rwkv_w2=rwkv_w2, rwkv_a0=rwkv_a0, rwkv_a2=rwkv_a2, rwkv_k_k=rwkv_k_k,
             rwkv_k_a=rwkv_k_a, rwkv_r_k=rwkv_r_k, rwkv_ln_w=rwkv_ln_w, rwkv_ln_b=rwkv_ln_b, s5_a_re=s5_a_re,
             s5_a_im=s5_a_im, s5_log_dt=s5_log_dt, s5_b_re=s5_b_re, s5_b_im=s5_b_im, s5_c_re=s5_c_re,
             s5_c_im=s5_c_im, s5_d=s5_d, s5_glu_w=s5_glu_w, s5_glu_b=s5_glu_b, w_branch=w_branch, w_gate=w_gate,
             b_gate=b_gate, w_out=w_out)
    layers = [_layer_params(l, a) for l in range(DEPTH)]
    meta = meta_tokens.astype(x_prompt.dtype)
    return (_trunk(x_prompt, meta, layers, final_norm_w), _trunk(x_sample, meta, layers, final_norm_w))
```

```python
import functools

import jax
import jax.numpy as jnp
from jax import lax
from jax.experimental import pallas as pl
from jax.experimental.pallas import tpu as pltpu

F32 = jnp.float32
BF16 = jnp.bfloat16
HI = lax.Precision.HIGHEST

D_MODEL = 2048
DEPTH = 2
N_META = 16
CHUNK = 64
TAIL_PAD = CHUNK - N_META
WIDTH = D_MODEL // 2
RMS_EPS = 1e-6
NEG = -1e30

MLSTM_HEADS = 4
MLSTM_DV = WIDTH // MLSTM_HEADS
MLSTM_DQK = MLSTM_DV // 2
HGRN_EXPAND = 128
HGRN_HEADS = WIDTH // HGRN_EXPAND
HGRN_SUB = 16
HGRN_GROUP = 8
RWKV_HEAD = 64
RWKV_HEADS = WIDTH // RWKV_HEAD
RWKV_LORA = 64
RWKV_GN_EPS = 64e-5
S5_GROUP = 16
S5_GROUPS = WIDTH // S5_GROUP
S5_STATE = 64
S5_BLOCK = 16
S5_ROW = S5_BLOCK * S5_GROUP

LANES = 128
VMEM_LIMIT = 48 * 1024 * 1024

C_QA, C_KA, C_VA, C_OA, C_GA = 0, 512, 1024, 2048, 3072
C_QB, C_FB, C_IB, C_GB = 4096, 5120, 7168, 8192
C_RC, C_KC, C_VC, C_GC = 9216, 10240, 11264, 12288
C_UD, C_GD = 13312, 14336
N_MAIN = 15360
N_SMALL = 384
G_OFF = 256


def _params(**kw):
    return pltpu.CompilerParams(vmem_limit_bytes=VMEM_LIMIT, **kw)


def _tile(total, cap, mult):
    best = None
    for t in range(mult, min(total, cap) + 1, mult):
        if total % t == 0:
            best = t
    assert best is not None, (total, cap, mult)
    return best


def _sigmoid(x):
    return jax.nn.sigmoid(x)


def _silu(x):
    return x * jax.nn.sigmoid(x)


def _softplus(x):
    return jnp.maximum(x, 0.0) + jnp.log(1.0 + jnp.exp(-jnp.abs(x)))


def _dot_nt(a, b, precision=None):
    return lax.dot_general(a, b, (((1,), (1,)), ((), ())), precision=precision, preferred_element_type=F32)


def _dot_tn(a, b, precision=None):
    return lax.dot_general(a, b, (((0,), (0,)), ((), ())), precision=precision, preferred_element_type=F32)


def _rmsnorm_body(x_ref, w_ref, o_ref):
    x = x_ref[...]
    y = x * lax.rsqrt(jnp.mean(x * x, axis=-1, keepdims=True) + RMS_EPS) * w_ref[...]
    o_ref[...] = y.astype(o_ref.dtype)


def _rmsnorm(x2d, w, out_dtype):
    m, d = x2d.shape
    tm = _tile(m, 512, 16)
    return pl.pallas_call(
        _rmsnorm_body,
        grid=(m // tm,),
        in_specs=[pl.BlockSpec((tm, d), lambda i: (i, 0)), pl.BlockSpec((1, d), lambda i: (0, 0))],
        out_specs=pl.BlockSpec((tm, d), lambda i: (i, 0)),
        out_shape=jax.ShapeDtypeStruct((m, d), out_dtype),
        compiler_params=_params(dimension_semantics=("parallel",)),
        name="rmsnorm",
    )(x2d, w.reshape(1, d).astype(F32))


def _mm_body(*refs, has_res):
    x_ref, w_ref = refs[0], refs[1]
    o_ref = refs[-1]
    acc = jnp.dot(x_ref[...], w_ref[...], preferred_element_type=F32)
    if has_res:
        acc = refs[2][...] + acc
    o_ref[...] = acc.astype(o_ref.dtype)


def _mm(x, w, res=None, out_dtype=F32, name="mm"):
    m, k = x.shape
    n = w.shape[1]
    tm = _tile(m, 1408, 16)
    tn = 1024 if n % 1024 == 0 else n
    in_specs = [pl.BlockSpec((tm, k), lambda i, j: (i, 0)), pl.BlockSpec((k, tn), lambda i, j: (0, j))]
    args = [x, w]
    if res is not None:
        in_specs.append(pl.BlockSpec((tm, tn), lambda i, j: (i, j)))
        args.append(res)
    return pl.pallas_call(
        functools.partial(_mm_body, has_res=res is not None),
        grid=(m // tm, n // tn),
        in_specs=in_specs,
        out_specs=pl.BlockSpec((tm, tn), lambda i, j: (i, j)),
        out_shape=jax.ShapeDtypeStruct((m, n), out_dtype),
        compiler_params=_params(dimension_semantics=("parallel", "parallel")),
        name=name,
    )(*args)


def _mlstm_body(qf_ref, kf_ref, vf_ref, zf_ref, qb_ref, kb_ref, vb_ref, zb_ref, bias_ref, of_ref, ob_ref,
                c_sc, n_sc, m_sc, *, nc, lreal):
    c = pl.program_id(1)

    @pl.when(c == 0)
    def _():
        c_sc[...] = jnp.zeros_like(c_sc)
        n_sc[...] = jnp.zeros_like(n_sc)
        m_sc[...] = jnp.zeros_like(m_sc)

    L = CHUNK
    row = lax.broadcasted_iota(jnp.int32, (L, 1), 0)
    lane = lax.broadcasted_iota(jnp.int32, (1, LANES), 1)
    is_f = (lane >= 2 * MLSTM_HEADS) & (lane < 4 * MLSTM_HEADS)
    r_i = lax.broadcasted_iota(jnp.int32, (L, L), 0)
    c_i = lax.broadcasted_iota(jnp.int32, (L, L), 1)
    valids, gs, causals, balls, qd, kd, vd = [], [], [], [], [], [], []
    for d, (q_ref, k_ref, v_ref, zs_ref) in enumerate(((qf_ref, kf_ref, vf_ref, zf_ref),
                                                       (qb_ref, kb_ref, vb_ref, zb_ref))):
        cc = nc - 1 - c if d else c
        valid = (cc * L + row) < lreal
        g = zs_ref[0][:, G_OFF:G_OFF + LANES] + bias_ref[:, G_OFF:G_OFF + LANES]
        lf_all = jnp.where(valid & is_f, -_softplus(-g), 0.0)
        causal = (c_i >= r_i) if d else (c_i <= r_i)
        valids.append(valid)
        gs.append(g)
        causals.append(causal)
        balls.append(jnp.dot(causal.astype(F32), lf_all, precision=HI, preferred_element_type=F32))
        qd.append(q_ref[0])
        kd.append(k_ref[0] * (MLSTM_DQK ** -0.5))
        vd.append(v_ref[0])
    o_refs = (of_ref, ob_ref)
    chains = [(d, h) for d in range(2) for h in range(MLSTM_HEADS)]
    bs, lis, dlogs = [], [], []
    for d, h in chains:
        fcol = 2 * MLSTM_HEADS + MLSTM_HEADS * d + h
        icol = MLSTM_HEADS * d + h
        b = jnp.sum(jnp.where(lane == fcol, balls[d], 0.0), axis=1, keepdims=True)
        li = jnp.sum(jnp.where(lane == icol, gs[d], 0.0), axis=1, keepdims=True)
        li = jnp.where(valids[d], li, NEG)
        fm = jnp.where(lane == 0, b, jnp.where(lane == 1, 1.0, 0.0))
        gm = jnp.where(lane == 0, 1.0, jnp.where(lane == 1, li - b, 0.0))
        bs.append(b)
        lis.append(li)
        dlogs.append(jnp.where(causals[d], _dot_nt(fm, gm, HI), NEG))
    m_sts = [m_sc[d, h][:, 0:1] for d, h in chains]
    avs = [b + m_st for b, m_st in zip(bs, m_sts)]
    mts = [jnp.maximum(a, jnp.max(dlog, axis=1, keepdims=True)) for a, dlog in zip(avs, dlogs)]
    qhs = [qd[d][:, h * MLSTM_DQK:(h + 1) * MLSTM_DQK] for d, h in chains]
    khs = [kd[d][:, h * MLSTM_DQK:(h + 1) * MLSTM_DQK] for d, h in chains]
    vhs = [vd[d][:, h * MLSTM_DV:(h + 1) * MLSTM_DV] for d, h in chains]
    qbs = [x.astype(BF16) for x in qhs]
    vbs = [x.astype(BF16) for x in vhs]
    sqk = [_dot_nt(qb, kh.astype(BF16)) for qb, kh in zip(qbs, khs)]
    qcs = [jnp.dot(qb, c_sc[d, h].astype(BF16), preferred_element_type=F32) for (d, h), qb in zip(chains, qbs)]
    ws = [jnp.exp(dlog - mt) * s for dlog, mt, s in zip(dlogs, mts, sqk)]
    wvs = [jnp.dot(w.astype(BF16), vb, preferred_element_type=F32) for w, vb in zip(ws, vbs)]
    for e, (d, h) in enumerate(chains):
        inter = jnp.exp(avs[e] - mts[e])
        num = inter * qcs[e] + wvs[e]
        den = inter * jnp.sum(qhs[e] * n_sc[d, h], axis=1, keepdims=True) + jnp.sum(ws[e], axis=1, keepdims=True)
        o_refs[d][0, :, h * MLSTM_DV:(h + 1) * MLSTM_DV] = num / jnp.maximum(jnp.abs(den), jnp.exp(-mts[e]))
    for e, (d, h) in enumerate(chains):
        b, li, m_st = bs[e], lis[e], m_sts[e]
        bl = b[0:1] if d else b[L - 1:L]
        tail = bl - b + li
        m_new = jnp.maximum(bl + m_st, jnp.max(tail, axis=0, keepdims=True))
        kw = khs[e] * jnp.exp(tail - m_new)
        decay = jnp.exp(bl + m_st - m_new)
        c_sc[d, h] = decay * c_sc[d, h] + _dot_tn(kw, vhs[e])
        n_sc[d, h] = decay * n_sc[d, h] + jnp.sum(kw, axis=0, keepdims=True)
        m_sc[d, h] = jnp.broadcast_to(m_new, (1, LANES))


def _mlstm_scan(zm, zs, small_bias, lreal):
    n, tp, _ = zm.shape
    nc = tp // CHUNK
    specs = []
    for cidx in (lambda c: c, lambda c: nc - 1 - c):
        specs += [
            pl.BlockSpec((1, CHUNK, 512), lambda i, c, cidx=cidx: (i, cidx(c), C_QA // 512)),
            pl.BlockSpec((1, CHUNK, 512), lambda i, c, cidx=cidx: (i, cidx(c), C_KA // 512)),
            pl.BlockSpec((1, CHUNK, WIDTH), lambda i, c, cidx=cidx: (i, cidx(c), C_VA // WIDTH)),
            pl.BlockSpec((1, CHUNK, N_SMALL), lambda i, c, cidx=cidx: (i, cidx(c), 0)),
        ]
    return pl.pallas_call(
        functools.partial(_mlstm_body, nc=nc, lreal=lreal),
        grid=(n, nc),
        in_specs=specs + [pl.BlockSpec((1, N_SMALL), lambda i, c: (0, 0))],
        out_specs=[pl.BlockSpec((1, CHUNK, WIDTH), lambda i, c: (i, c, 0)),
                   pl.BlockSpec((1, CHUNK, WIDTH), lambda i, c: (i, nc - 1 - c, 0))],
        out_shape=[jax.ShapeDtypeStruct((n, tp, WIDTH), F32)] * 2,
        scratch_shapes=[
            pltpu.VMEM((2, MLSTM_HEADS, MLSTM_DQK, MLSTM_DV), F32),
            pltpu.VMEM((2, MLSTM_HEADS, 1, MLSTM_DQK), F32),
            pltpu.VMEM((2, MLSTM_HEADS, 1, LANES), F32),
        ],
        compiler_params=_params(dimension_semantics=("parallel", "arbitrary")),
        name="mlstm_scan",
    )(zm, zm, zm, zs, zm, zm, zm, zs, small_bias)


def _mlstm_post_body(hf_ref, hb_ref, oa_ref, ga_ref, nw_ref, o_ref):
    h = hf_ref[0] + hb_ref[0]
    for j in range(MLSTM_HEADS):
        sl = slice(j * MLSTM_DV, (j + 1) * MLSTM_DV)
        hh = h[:, sl]
        hn = hh * lax.rsqrt(jnp.mean(hh * hh, axis=1, keepdims=True) + RMS_EPS) * nw_ref[:, sl]
        o_ref[0, :, sl] = (hn * _sigmoid(oa_ref[0][:, sl]) * _silu(ga_ref[0][:, sl])).astype(o_ref.dtype)


def _mlstm_post(hf, hb, zm, norm_w):
    n, tp, _ = hf.shape
    tr = _tile(tp, 512, 16)
    return pl.pallas_call(
        _mlstm_post_body,
        grid=(n, tp // tr),
        in_specs=[
            pl.BlockSpec((1, tr, WIDTH), lambda i, r: (i, r, 0)),
            pl.BlockSpec((1, tr, WIDTH), lambda i, r: (i, r, 0)),
            pl.BlockSpec((1, tr, WIDTH), lambda i, r: (i, r, C_OA // WIDTH)),
            pl.BlockSpec((1, tr, WIDTH), lambda i, r: (i, r, C_GA // WIDTH)),
            pl.BlockSpec((1, WIDTH), lambda i, r: (0, 0)),
        ],
        out_specs=pl.BlockSpec((1, tr, WIDTH), lambda i, r: (i, r, 0)),
        out_shape=jax.ShapeDtypeStruct((n, tp, WIDTH), BF16),
        compiler_params=_params(dimension_semantics=("parallel", "parallel")),
        name="mlstm_post",
    )(hf, hb, zm, zm, norm_w.reshape(1, WIDTH))


def _cumsum_rows(tri_bf16, x):
    hi = x.astype(BF16)
    r1 = x - hi.astype(F32)
    mid = r1.astype(BF16)
    lo = (r1 - mid.astype(F32)).astype(BF16)
    dot = lambda t: jnp.dot(tri_bf16, t, preferred_element_type=F32)
    return dot(hi) + dot(mid) + dot(lo)


def _hgrn_body(q_ref, f_ref, v_ref, lb_ref, o_ref, st_sc, *, nc, lreal, rev):
    c = pl.program_id(1)
    cc = nc - 1 - c if rev else c

    @pl.when(c == 0)
    def _():
        st_sc[...] = jnp.zeros_like(st_sc)

    L = CHUNK
    sub = 8
    row = lax.broadcasted_iota(jnp.int32, (L, 1), 0)
    valid = (cc * L + row) < lreal
    r_i = lax.broadcasted_iota(jnp.int32, (L, L), 0)
    c_i = lax.broadcasted_iota(jnp.int32, (L, L), 1)
    tri = ((c_i >= r_i) if rev else (c_i <= r_i)).astype(BF16)
    ones = jnp.ones((HGRN_EXPAND, HGRN_EXPAND), BF16)

    nblk = L // HGRN_SUB
    vpb = HGRN_SUB // sub

    def group(gi, carry):
        hs = [gi * HGRN_GROUP + u for u in range(HGRN_GROUP)]
        sls = [pl.ds(pl.multiple_of(h * HGRN_EXPAND, HGRN_EXPAND), HGRN_EXPAND) for h in hs]
        qs, kks, vs, bs = [], [], [], []
        for sl in sls:
            fpre = f_ref[0, :, sl]
            lb = lb_ref[0, :, sl]
            log_f = jnp.where(valid, jnp.log(lb + (1.0 - lb) * _sigmoid(fpre)), 0.0)
            qs.append(_silu(q_ref[0, :, sl]))
            kks.append(jnp.where(valid, (1.0 - lb) * _sigmoid(-fpre), 0.0))
            vs.append(v_ref[0, :, sl])
            bs.append(_cumsum_rows(tri, log_f))
        accs = [[jnp.zeros((sub, HGRN_EXPAND), F32) for _ in range(L // sub)] for _ in hs]
        for blk in range(nblk):
            spans = []
            for s in range(blk * HGRN_SUB, (blk + 1) * HGRN_SUB):
                lo_v, hi_v = (blk * vpb, s // sub + 1) if rev else (s // sub, (blk + 1) * vpb)
                spans.append((s, lo_v, hi_v))
            reds = []
            for q, kk, b in zip(qs, kks, bs):
                pieces = []
                for s, lo_v, hi_v in spans:
                    rows = slice(lo_v * sub, hi_v * sub)
                    seen = (row[rows] <= s) if rev else (row[rows] >= s)
                    pieces.append((q[rows] * kk[s:s + 1]) * jnp.exp(jnp.where(seen, b[rows] - b[s:s + 1], NEG)))
                reds.append(jnp.dot(jnp.concatenate(pieces, axis=0).astype(BF16), ones, preferred_element_type=F32))
            for acc, red, v in zip(accs, reds, vs):
                r0 = 0
                for s, lo_v, hi_v in spans:
                    for jv in range(lo_v, hi_v):
                        acc[jv] = acc[jv] + red[r0:r0 + sub] * v[s:s + 1]
                        r0 += sub
        for blk in range(nblk):
            src = slice((blk + 1) * HGRN_SUB, L) if rev else slice(0, blk * HGRN_SUB)
            if src.start == src.stop:
                continue
            tgt = slice(blk * HGRN_SUB, (blk + 1) * HGRN_SUB)
            edge = (blk + 1) * HGRN_SUB if rev else blk * HGRN_SUB - 1
            atts = []
            for q, kk, b in zip(qs, kks, bs):
                beta = b[edge:edge + 1]
                qt = (q[tgt] * jnp.exp(b[tgt] - beta)).astype(BF16)
                kt = (kk[src] * jnp.exp(beta - b[src])).astype(BF16)
                atts.append(_dot_nt(qt, kt).astype(BF16))
            for acc, att, v in zip(accs, atts, vs):
                o_blk = jnp.dot(att, v[src].astype(BF16), preferred_element_type=F32)
                for jv in range(vpb):
                    acc[blk * vpb + jv] = acc[blk * vpb + jv] + o_blk[jv * sub:(jv + 1) * sub]
        sts = [st_sc[h] for h in hs]
        inters = [_dot_nt((q * jnp.exp(b)).astype(BF16), st.astype(BF16)) for q, b, st in zip(qs, bs, sts)]
        for sl, acc, inter in zip(sls, accs, inters):
            o_ref[0, :, sl] = jnp.concatenate(acc, axis=0) + inter
        for h, st, v, kk, b in zip(hs, sts, vs, kks, bs):
            bl = b[0:1] if rev else b[L - 1:L]
            st_sc[h] = st * jnp.exp(bl) + _dot_tn(v, kk * jnp.exp(bl - b))
        return carry

    lax.fori_loop(0, HGRN_HEADS // HGRN_GROUP, group, 0)


def _hgrn_scan(zm, lb, lreal, rev):
    n, tp, _ = zm.shape
    nc = tp // CHUNK
    cidx = (lambda c: nc - 1 - c) if rev else (lambda c: c)
    return pl.pallas_call(
        functools.partial(_hgrn_body, nc=nc, lreal=lreal, rev=rev),
        grid=(n, nc),
        in_specs=[
            pl.BlockSpec((1, CHUNK, WIDTH), lambda i, c: (i, cidx(c), C_QB // WIDTH)),
            pl.BlockSpec((1, CHUNK, WIDTH), lambda i, c: (i, cidx(c), C_FB // WIDTH + int(rev))),
            pl.BlockSpec((1, CHUNK, WIDTH), lambda i, c: (i, cidx(c), C_IB // WIDTH)),
            pl.BlockSpec((1, 1, WIDTH), lambda i, c: (int(rev), 0, 0)),
        ],
        out_specs=pl.BlockSpec((1, CHUNK, WIDTH), lambda i, c: (i, cidx(c), 0)),
        out_shape=jax.ShapeDtypeStruct((n, tp, WIDTH), F32),
        scratch_shapes=[pltpu.VMEM((HGRN_HEADS, HGRN_EXPAND, HGRN_EXPAND), F32)],
        compiler_params=_params(dimension_semantics=("parallel", "arbitrary")),
        name="hgrn_scan_bwd" if rev else "hgrn_scan_fwd",
    )(zm, zm, zm, lb)


def _hgrn_post_body(of_ref, ob_ref, gb_ref, nw_ref, y_ref):
    o = of_ref[0] + ob_ref[0]
    for j in range(HGRN_HEADS):
        sl = slice(j * HGRN_EXPAND, (j + 1) * HGRN_EXPAND)
        oh = o[:, sl]
        on = oh * lax.rsqrt(jnp.mean(oh * oh, axis=1, keepdims=True) + RMS_EPS) * nw_ref[...]
        y_ref[0, :, sl] = (on * _silu(gb_ref[0][:, sl])).astype(y_ref.dtype)


def _hgrn_post(of, ob, zm, norm_w):
    n, tp, _ = of.shape
    tr = _tile(tp, 512, 16)
    spec = pl.BlockSpec((1, tr, WIDTH), lambda i, r: (i, r, 0))
    return pl.pallas_call(
        _hgrn_post_body,
        grid=(n, tp // tr),
        in_specs=[spec, spec,
                  pl.BlockSpec((1, tr, WIDTH), lambda i, r: (i, r, C_GB // WIDTH)),
                  pl.BlockSpec((1, HGRN_EXPAND), lambda i, r: (0, 0))],
        out_specs=spec,
        out_shape=jax.ShapeDtypeStruct((n, tp, WIDTH), BF16),
        compiler_params=_params(dimension_semantics=("parallel", "parallel")),
        name="hgrn_post",
    )(of, ob, zm, norm_w.reshape(1, HGRN_EXPAND))


def _shift_body(x_ref, mu_ref, o_ref, *, lreal):
    x = x_ref[0]
    tp = x.shape[0]
    row = lax.broadcasted_iota(jnp.int32, (tp, 1), 0)
    prev = jnp.where(row == 0, 0.0, pltpu.roll(x, 1, 0))
    nxt = jnp.where(row + 1 >= lreal, 0.0, pltpu.roll(x, tp - 1, 0))
    o_ref[0] = x + mu_ref[0:1, :] * (prev - x) + mu_ref[1:2, :] * (nxt - x)


def _token_shift(z, col0, ncols, mu, lreal):
    n, tp, _ = z.shape
    tc = LANES
    return pl.pallas_call(
        functools.partial(_shift_body, lreal=lreal),
        grid=(n, ncols // tc),
        in_specs=[
            pl.BlockSpec((1, tp, tc), lambda i, j: (i, 0, col0 // tc + j)),
            pl.BlockSpec((2, tc), lambda i, j: (0, j)),
        ],
        out_specs=pl.BlockSpec((1, tp, tc), lambda i, j: (i, 0, j)),
        out_shape=jax.ShapeDtypeStruct((n, tp, ncols), F32),
        compiler_params=_params(dimension_semantics=("parallel", "parallel")),
        name="token_shift",
    )(z, mu)


def _seg_ones():
    r = lax.broadcasted_iota(jnp.int32, (LANES, LANES), 0) // RWKV_HEAD
    c = lax.broadcasted_iota(jnp.int32, (LANES, LANES), 1) // RWKV_HEAD
    return (r == c).astype(BF16)


def _head_sum(x, seg):
    hi = x.astype(BF16)
    r1 = x - hi.astype(F32)
    mid = r1.astype(BF16)
    lo = (r1 - mid.astype(F32)).astype(BF16)
    parts = []
    for j in range(WIDTH // LANES):
        sl = slice(j * LANES, (j + 1) * LANES)
        parts.append(jnp.dot(hi[:, sl], seg, preferred_element_type=F32)
                     + jnp.dot(mid[:, sl], seg, preferred_element_type=F32)
                     + jnp.dot(lo[:, sl], seg, preferred_element_type=F32))
    return jnp.concatenate(parts, axis=1)


def _rwkv_pre_body(f_ref, fs_ref, w2_ref, a2_ref, w0_ref, a0_ref, kk_ref, ka_ref, rk_ref,
                   r_o, v_o, kk_o, wf_o, wb_o, nf_o, nb_o, kf_o, kb_o, bonus_o, *, lreal):
    f = f_ref[0]
    r = f[:, 0:WIDTH]
    k = f[:, WIDTH:2 * WIDTH]
    v = f[:, 2 * WIDTH:3 * WIDTH]
    fs = fs_ref[0]
    wl = fs[:, 0:LANES]
    al = fs[:, LANES:2 * LANES]
    seg = _seg_ones()
    w = w0_ref[...] + jnp.dot(jnp.tanh(wl).astype(BF16), w2_ref[...], preferred_element_type=F32)
    wdec = jnp.exp(-jnp.exp(-_softplus(-w) - 0.5))
    a = _sigmoid(a0_ref[...] + jnp.dot(al.astype(BF16), a2_ref[...], preferred_element_type=F32))
    kk = k * kk_ref[...]
    kk = kk * jnp.minimum(lax.rsqrt(_head_sum(kk * kk, seg)), 1e12)
    a_f, a_b = a[:, 0:WIDTH], a[:, WIDTH:2 * WIDTH]
    kd_f = k * (1.0 + (a_f - 1.0) * ka_ref[...])
    kd_b = k * (1.0 + (a_b - 1.0) * ka_ref[...])
    tr = f.shape[0]
    valid = (pl.program_id(1) * tr + lax.broadcasted_iota(jnp.int32, (tr, 1), 0)) < lreal
    keep = lambda x: jnp.where(valid, x, 0.0)

    def put(o_ref, x):
        npair = WIDTH // LANES
        for p in range(npair):
            o_ref[0, pl.ds(p, tr, stride=npair), :] = x[:, p * LANES:(p + 1) * LANES]

    put(r_o, keep(r))
    put(v_o, keep(v))
    put(kk_o, keep(kk))
    put(wf_o, jnp.where(valid, wdec[:, 0:WIDTH], 1.0))
    put(wb_o, jnp.where(valid, wdec[:, WIDTH:2 * WIDTH], 1.0))
    put(nf_o, keep(-(kk * a_f)))
    put(nb_o, keep(-(kk * a_b)))
    put(kf_o, keep(kd_f))
    put(kb_o, keep(kd_b))
    bonus_o[0] = _head_sum(r * (kd_f + kd_b) * rk_ref[...], seg) * v


def _rwkv_pre(fsh, fsm, w2cat, a2cat, w0, a0, k_k, k_a, r_k, lreal):
    n, tp, _ = fsh.shape
    tr = _tile(tp, 256, 8)
    row = lambda width: pl.BlockSpec((1, width), lambda i, r: (0, 0))
    out_spec = pl.BlockSpec((1, tr, WIDTH), lambda i, r: (i, r, 0))
    npair = WIDTH // LANES
    outs = pl.pallas_call(
        functools.partial(_rwkv_pre_body, lreal=lreal),
        grid=(n, tp // tr),
        in_specs=[
            pl.BlockSpec((1, tr, 3 * WIDTH), lambda i, r: (i, r, 0)),
            pl.BlockSpec((1, tr, 2 * LANES), lambda i, r: (i, r, 0)),
            pl.BlockSpec((LANES, 2 * WIDTH), lambda i, r: (0, 0)),
            pl.BlockSpec((LANES, 2 * WIDTH), lambda i, r: (0, 0)),
            row(2 * WIDTH), row(2 * WIDTH), row(WIDTH), row(WIDTH), row(WIDTH),
        ],
        out_specs=[pl.BlockSpec((1, tr * npair, LANES), lambda i, r: (i, r, 0))] * 9 + [out_spec],
        out_shape=[jax.ShapeDtypeStruct((n, tp * npair, LANES), F32)] * 9
        + [jax.ShapeDtypeStruct((n, tp, WIDTH), F32)],
        compiler_params=_params(dimension_semantics=("parallel", "parallel")),
        name="rwkv_pre",
    )(fsh, fsm, w2cat, a2cat, w0, a0, k_k, k_a, r_k)
    return [o.reshape(n, tp, npair, LANES) for o in outs[:9]] + [outs[9]]


RWKV_PAIRS = WIDTH // LANES
RWKV_TB = 32
RWKV_NV1 = 8


def _sum_interleaved(terms, nacc):
    accs = []
    for i, t in enumerate(terms):
        if i < nacc:
            accs.append(t)
        else:
            accs[i % nacc] = accs[i % nacc] + t
    while len(accs) > 1:
        accs = [accs[i] + accs[i + 1] for i in range(0, len(accs), 2)]
    return accs[0]


def _rwkv_chain_step(s_sc, t_sc, load_v, g, nacc, nsplit=1):
    base = g * RWKV_HEAD
    row = lambda i, k: t_sc[i, base + k:base + k + 1, :]
    nv = s_sc.shape[2]
    part = nv // nsplit
    outs = []
    for h in range(nsplit):
        rs = slice(h * part, (h + 1) * part)
        sa = _sum_interleaved((s_sc[g, k, rs, :] * row(1, k) for k in range(RWKV_HEAD)), nacc)
        v_h = load_v()[rs]

        def update(k):
            s = s_sc[g, k, rs, :] * row(0, k) + sa * row(2, k) + v_h * row(3, k)
            s_sc[g, k, rs, :] = s
            return s * row(4, k)

        outs.append(_sum_interleaved((update(k) for k in range(RWKV_HEAD)), nacc))
    return outs[0] if nsplit == 1 else jnp.concatenate(outs, axis=0)


def _rwkv_pipelined_steps(build, compute, t_a, t_b):
    build(0, t_a)

    def pair(j2, carry):
        j = 2 * j2
        build(j + 1, t_b)
        compute(j, t_a)
        build(jnp.minimum(j + 2, RWKV_TB - 1), t_a)
        compute(j + 1, t_b)
        return carry

    lax.fori_loop(0, RWKV_TB // 2, pair, 0)


def _rwkv_scan8_body(wf, wb, kkf, kkb, nf, nb, kdf, kdb, rf, rb, vf, vb, yf_ref, yb_ref, s_sc, t_a, t_b):
    @pl.when(pl.program_id(0) == 0)
    def _():
        s_sc[...] = jnp.zeros_like(s_sc)

    nseq = wf.shape[0]
    operands = ((wf, wb), (kkf, kkb), (nf, nb), (kdf, kdb), (rf, rb), (vf, vb))

    def build(j, t_sc):
        jb = RWKV_TB - 1 - j
        for i, (xf_ref, xb_ref) in enumerate(operands):
            rows = [xf_ref[s, j] for s in range(nseq)] + [xb_ref[s, jb] for s in range(nseq)]
            t_sc[i] = jnp.concatenate(rows, axis=0).T

    def compute(j, t_sc):
        jb = RWKV_TB - 1 - j
        ys = [_rwkv_chain_step(s_sc, t_sc, lambda g=g: t_sc[5, g * RWKV_HEAD:(g + 1) * RWKV_HEAD, :], g, 4, 2)
              for g in range(2)]
        yt = jnp.concatenate(ys, axis=0).T
        for s in range(nseq):
            yf_ref[s, j] = yt[s * RWKV_PAIRS:(s + 1) * RWKV_PAIRS]
            yb_ref[s, jb] = yt[(nseq + s) * RWKV_PAIRS:(nseq + s + 1) * RWKV_PAIRS]

    _rwkv_pipelined_steps(build, compute, t_a, t_b)


def _rwkv_scan1_body(wf, wb, kkf, kkb, nf, nb, kdf, kdb, rf, rb, vf, vb, yf_ref, yb_ref, s_sc, t_a, t_b):
    @pl.when(pl.program_id(0) == 0)
    def _():
        s_sc[...] = jnp.zeros_like(s_sc)

    nv = RWKV_NV1
    noct = RWKV_HEAD // nv
    operands = ((wf, wb), (kkf, kkb), (nf, nb), (kdf, kdb), (rf, rb), (vf, vb))

    def build(j, t_sc):
        jb = RWKV_TB - 1 - j
        for i, (xf_ref, xb_ref) in enumerate(operands):
            rows = [jnp.broadcast_to(xf_ref[0, j, p:p + 1, :], (8, LANES)) for p in range(RWKV_PAIRS)]
            rows += [jnp.broadcast_to(xb_ref[0, jb, p:p + 1, :], (8, LANES)) for p in range(RWKV_PAIRS)]
            t_sc[i] = jnp.concatenate(rows, axis=0).T

    def compute(j, t_sc):
        jb = RWKV_TB - 1 - j
        octet = lax.broadcasted_iota(jnp.int32, (nv, LANES), 1) % noct
        ys = []
        for g in range(2):
            v_g = jnp.zeros((nv, LANES), F32)
            for q in range(noct):
                r0 = g * RWKV_HEAD + q * nv
                v_g = jnp.where(octet == q, t_sc[5, r0:r0 + nv, :], v_g)
            ys.append(_rwkv_chain_step(s_sc, t_sc, lambda v_g=v_g: v_g, g, 8))
        zrows = [jnp.where(octet == q, ys[g], 0.0) for g in range(2) for q in range(noct)]
        yt = jnp.concatenate(zrows, axis=0).T
        for dp in range(2 * RWKV_PAIRS):
            rowv = jnp.sum(yt[dp * 8:(dp + 1) * 8], axis=0, keepdims=True)
            p = dp % RWKV_PAIRS
            if dp < RWKV_PAIRS:
                yf_ref[0, j, p:p + 1, :] = rowv
            else:
                yb_ref[0, jb, p:p + 1, :] = rowv

    _rwkv_pipelined_steps(build, compute, t_a, t_b)


def _rwkv_scan(wf, wb, kk, nf, nb, kdf, kdb, r, v):
    n, tp = r.shape[:2]
    nblk = tp // RWKV_TB
    fspec = pl.BlockSpec((n, RWKV_TB, RWKV_PAIRS, LANES), lambda i: (0, i, 0, 0))
    bspec = pl.BlockSpec((n, RWKV_TB, RWKV_PAIRS, LANES), lambda i: (0, nblk - 1 - i, 0, 0))
    if n == 8:
        body, nv = _rwkv_scan8_body, RWKV_HEAD
    else:
        assert n == 1
        body, nv = _rwkv_scan1_body, RWKV_NV1
    return pl.pallas_call(
        body,
        grid=(nblk,),
        in_specs=[fspec, bspec] * 6,
        out_specs=[fspec, bspec],
        out_shape=[jax.ShapeDtypeStruct((n, tp, RWKV_PAIRS, LANES), F32)] * 2,
        scratch_shapes=[pltpu.VMEM((2, RWKV_HEAD, nv, LANES), F32),
                        pltpu.VMEM((6, LANES, LANES), F32), pltpu.VMEM((6, LANES, LANES), F32)],
        compiler_params=_params(dimension_semantics=("arbitrary",)),
        name="rwkv_scan",
    )(wf, wb, kk, kk, nf, nb, kdf, kdb, r, r, v, v)


def _rwkv_post_body(yf_ref, yb_ref, bonus_ref, gc_ref, lnw_ref, lnb_ref, o_ref):
    seg = _seg_ones()
    tr = o_ref.shape[1]
    rows = lambda p: pl.ds(p, tr, stride=RWKV_PAIRS)
    y = jnp.concatenate([yf_ref[0, rows(p), :] + yb_ref[0, rows(p), :] for p in range(RWKV_PAIRS)], axis=1)
    mean = _head_sum(y, seg) * (1.0 / RWKV_HEAD)
    yc = y - mean
    var = _head_sum(yc * yc, seg) * (1.0 / RWKV_HEAD)
    yn = yc * lax.rsqrt(var + RWKV_GN_EPS) * lnw_ref[...] + lnb_ref[...]
    o_ref[0] = ((yn + bonus_ref[0]) * _silu(gc_ref[0])).astype(o_ref.dtype)


def _rwkv_post(yf, yb, bonus, zm, ln_w, ln_b):
    n, tp = yf.shape[:2]
    tr = _tile(tp, 512, 16)
    spec = pl.BlockSpec((1, tr, WIDTH), lambda i, r: (i, r, 0))
    yspec = pl.BlockSpec((1, tr * RWKV_PAIRS, LANES), lambda i, r: (i, r, 0))
    yf, yb = (y.reshape(n, tp * RWKV_PAIRS, LANES) for y in (yf, yb))
    row = pl.BlockSpec((1, WIDTH), lambda i, r: (0, 0))
    return pl.pallas_call(
        _rwkv_post_body,
        grid=(n, tp // tr),
        in_specs=[yspec, yspec, spec, pl.BlockSpec((1, tr, WIDTH), lambda i, r: (i, r, C_GC // WIDTH)), row, row],
        out_specs=spec,
        out_shape=jax.ShapeDtypeStruct((n, tp, WIDTH), BF16),
        compiler_params=_params(dimension_semantics=("parallel", "parallel")),
        name="rwkv_post",
    )(yf, yb, bonus, zm, ln_w.reshape(1, WIDTH), ln_b.reshape(1, WIDTH))


def _rwkv_branch(zm, zs, p, lreal):
    fsh = _token_shift(zm, C_RC, 3 * WIDTH, p['mu_main'], lreal)
    fsm = _token_shift(zs, 0, 2 * LANES, p['mu_small'], lreal)
    r, v, kk, wf, wb, nf, nb, kf, kb, bonus = _rwkv_pre(
        fsh, fsm, p['w2cat'], p['a2cat'], p['w0'], p['a0'], p['k_k'], p['k_a'], p['r_k'], lreal)
    yf, yb = _rwkv_scan(wf, wb, kk, nf, nb, kf, kb, r, v)
    return _rwkv_post(yf, yb, bonus, zm, p['ln_w'], p['ln_b'])


def _s5_core_body(u_ref, m_ref, n_ref, q_ref, pw_ref, cidx_ref, y_ref, *, nsteps, nvalid, nblocks):
    cidx = cidx_ref[...]
    u = jnp.where(cidx < nvalid, u_ref[0], 0.0).astype(BF16)
    rows = u.shape[0]
    y = None
    for d in range(2):
        yd = jnp.dot(u, m_ref[d, 0], preferred_element_type=F32)
        x = jnp.dot(u, n_ref[d, 0], preferred_element_type=F32)
        for j in range(nsteps):
            s = 1 << j
            if d == 0:
                xs = jnp.where(cidx >= s, pltpu.roll(x, s, 0), 0.0)
            else:
                xs = jnp.where(cidx + s < nblocks, pltpu.roll(x, rows - s, 0), 0.0)
            x = (x + xs * pw_ref[d, 0, 2 * j:2 * j + 1, :]
                 + pltpu.roll(xs, S5_STATE, 1) * pw_ref[d, 0, 2 * j + 1:2 * j + 2, :])
        if d == 0:
            xin = jnp.where(cidx >= 1, pltpu.roll(x, 1, 0), 0.0)
        else:
            xin = jnp.where(cidx + 1 < nblocks, pltpu.roll(x, rows - 1, 0), 0.0)
        yd = yd + jnp.dot(xin.astype(BF16), q_ref[d, 0], preferred_element_type=F32)
        y = yd if y is None else y + yd
    y_ref[0] = y


def _s5_core(u, mats, cidx, nsteps, nvalid, nblocks):
    g, rows, _ = u.shape
    mm, nn, qq, pw = mats
    return pl.pallas_call(
        functools.partial(_s5_core_body, nsteps=nsteps, nvalid=nvalid, nblocks=nblocks),
        grid=(g,),
        in_specs=[
            pl.BlockSpec((1, rows, S5_ROW), lambda j: (j, 0, 0)),
            pl.BlockSpec((2, 1, S5_ROW, S5_ROW), lambda j: (0, j, 0, 0)),
            pl.BlockSpec((2, 1, S5_ROW, 2 * S5_STATE), lambda j: (0, j, 0, 0)),
            pl.BlockSpec((2, 1, 2 * S5_STATE, S5_ROW), lambda j: (0, j, 0, 0)),
            pl.BlockSpec((2, 1, pw.shape[2], 2 * S5_STATE), lambda j: (0, j, 0, 0)),
            pl.BlockSpec((rows, 1), lambda j: (0, 0)),
        ],
        out_specs=pl.BlockSpec((1, rows, S5_ROW), lambda j: (j, 0, 0)),
        out_shape=jax.ShapeDtypeStruct((g, rows, S5_ROW), F32),
        compiler_params=_params(dimension_semantics=("parallel",)),
        name="s5_core",
    )(u, mm, nn, qq, pw, cidx)


def _s5_post_body(y_ref, ud_ref, gd_ref, d_ref, w_ref, b_ref, o_ref):
    y = y_ref[0] + d_ref[...] * ud_ref[0]
    g = 0.5 * y * (1.0 + jnp.tanh(0.7978845608028654 * (y + 0.044715 * (y * y * y))))
    glu = jnp.dot(g.astype(BF16), w_ref[...], preferred_element_type=F32) + b_ref[...]
    o_ref[0] = (g * _sigmoid(glu) * _silu(gd_ref[0])).astype(o_ref.dtype)


def _s5_post(ys, zm, d, glu_w, glu_b):
    n, tp, _ = ys.shape
    tr = _tile(tp, 512, 16)
    spec = pl.BlockSpec((1, tr, WIDTH), lambda i, r: (i, r, 0))
    row = pl.BlockSpec((1, WIDTH), lambda i, r: (0, 0))
    return pl.pallas_call(
        _s5_post_body,
        grid=(n, tp // tr),
        in_specs=[spec,
                  pl.BlockSpec((1, tr, WIDTH), lambda i, r: (i, r, C_UD // WIDTH)),
                  pl.BlockSpec((1, tr, WIDTH), lambda i, r: (i, r, C_GD // WIDTH)),
                  row, pl.BlockSpec((WIDTH, WIDTH), lambda i, r: (0, 0)), row],
        out_specs=spec,
        out_shape=jax.ShapeDtypeStruct((n, tp, WIDTH), BF16),
        compiler_params=_params(dimension_semantics=("parallel", "parallel")),
        name="s5_post",
    )(ys, zm, zm, d.reshape(1, WIDTH), glu_w, glu_b.reshape(1, WIDTH))


def _s5_mats(a_re, a_im, log_dt, b_re, b_im, c_re, c_im, max_blocks):
    dt = jnp.exp(log_dt)[..., None]
    mag = jnp.exp(a_re * dt)
    ang = a_im * dt
    ab_re, ab_im = mag * jnp.cos(ang), mag * jnp.sin(ang)
    den = a_re * a_re + a_im * a_im
    xr, yi = ab_re - 1.0, ab_im
    coef_re = (xr * a_re + yi * a_im) / den
    coef_im = (yi * a_re - xr * a_im) / den
    bb_re = coef_re[..., None] * b_re - coef_im[..., None] * b_im
    bb_im = coef_re[..., None] * b_im + coef_im[..., None] * b_re
    pr, pi = [jnp.ones_like(ab_re)], [jnp.zeros_like(ab_re)]
    for _ in range(S5_BLOCK):
        pr, pi = pr + [pr[-1] * ab_re - pi[-1] * ab_im], pi + [pr[-1] * ab_im + pi[-1] * ab_re]
    pw_re, pw_im = jnp.stack(pr), jnp.stack(pi)
    t_re = pw_re[..., None] * bb_re - pw_im[..., None] * bb_im
    t_im = pw_re[..., None] * bb_im + pw_im[..., None] * bb_re
    kj = (jnp.einsum('dgcp,jdgpe->jdgce', c_re, t_re, precision=HI)
          - jnp.einsum('dgcp,jdgpe->jdgce', c_im, t_im, precision=HI))
    s_i = jnp.arange(S5_BLOCK)[:, None]
    i_i = jnp.arange(S5_BLOCK)[None, :]
    rev = jnp.arange(S5_BLOCK - 1, -1, -1)

    def tables(d):
        dist = (s_i - i_i) if d else (i_i - s_i)
        m6 = jnp.where((dist >= 0)[:, :, None, None, None], kj[jnp.clip(dist, 0, S5_BLOCK), d], 0.0)
        mm_d = jnp.transpose(m6, (2, 0, 4, 1, 3)).reshape(S5_GROUPS, S5_ROW, S5_ROW)
        order = jnp.arange(S5_BLOCK) if d else rev
        n_re = jnp.transpose(t_re[order, d], (1, 0, 3, 2)).reshape(S5_GROUPS, S5_ROW, S5_STATE)
        n_im = jnp.transpose(t_im[order, d], (1, 0, 3, 2)).reshape(S5_GROUPS, S5_ROW, S5_STATE)
        steps_in = (rev + 1) if d else (jnp.arange(S5_BLOCK) + 1)
        pr_i, pi_i = pw_re[steps_in, d][:, :, None, :], pw_im[steps_in, d][:, :, None, :]
        ca_re = c_re[d][None] * pr_i - c_im[d][None] * pi_i
        ca_im = c_re[d][None] * pi_i + c_im[d][None] * pr_i
        q_re = jnp.transpose(ca_re, (1, 3, 0, 2)).reshape(S5_GROUPS, S5_STATE, S5_ROW)
        q_im = jnp.transpose(-ca_im, (1, 3, 0, 2)).reshape(S5_GROUPS, S5_STATE, S5_ROW)
        return mm_d, jnp.concatenate([n_re, n_im], axis=-1), jnp.concatenate([q_re, q_im], axis=1)

    mm, nn, qq = (jnp.stack(t).astype(BF16) for t in zip(tables(0), tables(1)))
    sr, si = pw_re[S5_BLOCK], pw_im[S5_BLOCK]
    rows = []
    steps = 0
    while (1 << steps) < max_blocks:
        rows += [jnp.concatenate([sr, sr], axis=-1), jnp.concatenate([-si, si], axis=-1)]
        sr, si = sr * sr - si * si, 2.0 * sr * si
        steps += 1
    while len(rows) % 8:
        rows.append(jnp.zeros_like(rows[0]))
    pw = jnp.stack(rows, axis=2)
    return (mm, nn, qq, pw), steps


def _s5_branch(zm, p, lreal):
    n, tp, _ = zm.shape
    assert lreal % S5_BLOCK == 0 and tp % S5_BLOCK == 0
    nvalid = lreal // S5_BLOCK
    nb = tp // S5_BLOCK
    rows = n * nb
    rows_p = -(-rows // 8) * 8
    u = zm[:, :, C_UD:C_UD + WIDTH].reshape(n, nb, S5_BLOCK, S5_GROUPS, S5_GROUP)
    u = jnp.transpose(u, (3, 0, 1, 2, 4)).reshape(S5_GROUPS, rows, S5_ROW)
    u = jnp.pad(u, ((0, 0), (0, rows_p - rows), (0, 0)))
    cidx = jnp.pad(jnp.tile(jnp.arange(nb, dtype=jnp.int32), n), (0, rows_p - rows),
                   constant_values=nb).reshape(rows_p, 1)
    mats, nsteps = p['s5_mats'](nb)
    y = _s5_core(u, mats, cidx, nsteps, nvalid, nb)[:, :rows]
    ys = jnp.transpose(y.reshape(S5_GROUPS, n, nb, S5_BLOCK, S5_GROUP), (1, 2, 3, 0, 4)).reshape(n, tp, WIDTH)
    return _s5_post(ys, zm, p['s5_d'], p['glu_w'], p['glu_b'])


def _merge_body(h_ref, ya_ref, yb_ref, yc_ref, yd_ref, wg_ref, bg_ref, wb_ref, o_ref):
    h = h_ref[...]
    acc = None
    for b, y_ref in enumerate((ya_ref, yb_ref, yc_ref, yd_ref)):
        gate = _sigmoid(jnp.dot(h, wg_ref[b], preferred_element_type=F32) + bg_ref[b])
        term = gate * jnp.dot(y_ref[...], wb_ref[b], preferred_element_type=F32)
        acc = term if acc is None else acc + term
    o_ref[...] = acc.astype(o_ref.dtype)


def _merge(h, ys, wg, bg, wb):
    m, d = h.shape
    tm = _tile(m, 1024, 16)
    tn = 256
    yspec = pl.BlockSpec((tm, WIDTH), lambda i, j: (i, 0))
    return pl.pallas_call(
        _merge_body,
        grid=(m // tm, d // tn),
        in_specs=[pl.BlockSpec((tm, d), lambda i, j: (i, 0)), yspec, yspec, yspec, yspec,
                  pl.BlockSpec((4, d, tn), lambda i, j: (0, 0, j)),
                  pl.BlockSpec((4, 1, tn), lambda i, j: (0, 0, j)),
                  pl.BlockSpec((4, WIDTH, tn), lambda i, j: (0, 0, j))],
        out_specs=pl.BlockSpec((tm, tn), lambda i, j: (i, j)),
        out_shape=jax.ShapeDtypeStruct((m, d), BF16),
        compiler_params=_params(dimension_semantics=("parallel", "parallel")),
        name="merge",
    )(h, *ys, wg, bg, wb)


def _layer(xp, p, lreal):
    n, tp, d = xp.shape
    m = n * tp
    x2 = xp.reshape(m, d)
    h = _rmsnorm(x2, p['norm_w'], BF16)
    zm = _mm(h, p['w_main'], name="proj_main").reshape(n, tp, N_MAIN)
    zs = _mm(h, p['w_small'], name="proj_small").reshape(n, tp, N_SMALL)
    ya = _mlstm_post(*_mlstm_scan(zm, zs, p['small_bias'], lreal), zm, p['mlstm_norm_w'])
    yb = _hgrn_post(_hgrn_scan(zm, p['hgrn_lb'], lreal, False), _hgrn_scan(zm, p['hgrn_lb'], lreal, True), zm,
                    p['hgrn_norm_w'])
    yc = _rwkv_branch(zm, zs, p, lreal)
    yd = _s5_branch(zm, p, lreal)
    ys = [y.reshape(m, WIDTH) for y in (ya, yb, yc, yd)]
    merged = _merge(h, ys, p['w_gate'], p['b_gate'], p['w_branch'])
    return _mm(merged, p['w_out'], res=x2, name="proj_out").reshape(n, tp, d)


def _trunk(x, meta, layers, final_norm_w):
    n, t, d = x.shape
    lreal = t + N_META
    tp = lreal + TAIL_PAD
    xp = jnp.concatenate([jnp.broadcast_to(meta[None], (n, N_META, d)), x, jnp.zeros((n, TAIL_PAD, d), x.dtype)], axis=1)
    for p in layers:
        xp = _layer(xp, p, lreal)
    y = _rmsnorm(xp.reshape(n * tp, d), final_norm_w, F32).reshape(n, tp, d)
    return y[:, N_META:lreal]


def _layer_params(l, a):
    w_in = a['w_in'][l]
    offs = {}
    acc = 0
    sizes = (512, 512, WIDTH, WIDTH, 8, 8, WIDTH, WIDTH, 2 * WIDTH, WIDTH, WIDTH, WIDTH, WIDTH, WIDTH,
             2 * RWKV_LORA, 2 * RWKV_LORA, WIDTH, WIDTH, WIDTH)
    names = ('qa', 'ka', 'va', 'oa', 'iga', 'fga', 'ga', 'qb', 'fb', 'ib', 'gb', 'rc', 'kc', 'vc', 'wlc', 'alc',
             'gc', 'ud', 'gd')
    for nm, sz in zip(names, sizes):
        offs[nm] = (acc, acc + sz)
        acc += sz
    col = lambda nm: w_in[:, offs[nm][0]:offs[nm][1]]
    main_order = ('qa', 'ka', 'va', 'oa', 'ga', 'qb', 'fb', 'ib', 'gb', 'rc', 'kc', 'vc', 'gc', 'ud', 'gd')
    w_main = jnp.concatenate([col(nm) for nm in main_order], axis=1).astype(BF16)
    w_small = jnp.concatenate([col('wlc'), col('alc'), col('iga'), col('fga'),
                               jnp.zeros((D_MODEL, LANES - 16), F32)], axis=1).astype(BF16)
    small_bias = jnp.concatenate([jnp.zeros((G_OFF,), F32), a['mlstm_ig_b'][l].reshape(-1),
                                  a['mlstm_fg_b'][l].reshape(-1), jnp.zeros((LANES - 16,), F32)]).reshape(1, N_SMALL)
    lbw = jax.nn.softmax(a['hgrn_lower_bounds'].astype(F32), axis=1)
    hgrn_lb = (jnp.cumsum(lbw, axis=1) - lbw[:, :1])[:, l].reshape(2, 1, WIDTH)
    mu = a['rwkv_shift_mu'][l]
    zero = jnp.zeros((RWKV_LORA, WIDTH), F32)
    blockdiag = lambda w2: jnp.concatenate([jnp.concatenate([w2[0], zero], axis=1),
                                            jnp.concatenate([zero, w2[1]], axis=1)], axis=0)
    s5_args = tuple(a[k][l].astype(F32) for k in ('s5_a_re', 's5_a_im', 's5_log_dt', 's5_b_re', 's5_b_im',
                                                  's5_c_re', 's5_c_im'))
    return {
        'norm_w': a['norm_w'][l], 'w_main': w_main, 'w_small': w_small, 'small_bias': small_bias,
        'mlstm_norm_w': a['mlstm_norm_w'][l], 'hgrn_lb': hgrn_lb, 'hgrn_norm_w': a['hgrn_norm_w'][l],
        'mu_main': mu[:, :3 * WIDTH], 'mu_small': mu[:, 3 * WIDTH:],
        'w2cat': blockdiag(a['rwkv_w2'][l]).astype(BF16), 'a2cat': blockdiag(a['rwkv_a2'][l]).astype(BF16),
        'w0': a['rwkv_w0'][l].reshape(1, 2 * WIDTH), 'a0': a['rwkv_a0'][l].reshape(1, 2 * WIDTH),
        'k_k': a['rwkv_k_k'][l].reshape(1, WIDTH), 'k_a': a['rwkv_k_a'][l].reshape(1, WIDTH),
        'r_k': a['rwkv_r_k'][l].reshape(1, WIDTH),
        'ln_w': a['rwkv_ln_w'][l], 'ln_b': a['rwkv_ln_b'][l],
        's5_mats': functools.partial(_s5_mats, *s5_args),
        's5_d': a['s5_d'][l], 'glu_w': a['s5_glu_w'][l].astype(BF16), 'glu_b': a['s5_glu_b'][l],
        'w_gate': a['w_gate'][l].astype(BF16), 'b_gate': a['b_gate'][l].reshape(4, 1, D_MODEL),
        'w_branch': a['w_branch'][l].astype(BF16), 'w_out': a['w_out'][l].astype(BF16),
    }


def kernel(x_prompt, x_sample, meta_tokens, norm_w, w_in, mlstm_ig_b, mlstm_fg_b, mlstm_norm_w, hgrn_lower_bounds, hgrn_norm_w, rwkv_shift_mu, rwkv_w0, rwkv_w2, rwkv_a0, rwkv_a2, rwkv_k_k, rwkv_k_a, rwkv_r_k, rwkv_ln_w, rwkv_ln_b, s5_a_re, s5_a_im, s5_log_dt, s5_b_re, s5_b_im, s5_c_re, s5_c_im, s5_d, s5_glu_w, s5_glu_b, w_branch, w_gate, b_gate, w_out, final_norm_w):
    a = dict(norm_w=norm_w, w_in=w_in, mlstm_ig_b=mlstm_ig_b, mlstm_fg_b=mlstm_fg_b, mlstm_norm_w=mlstm_norm_w,
             hgrn_lower_bounds=hgrn_lower_bounds, hgrn_norm_w=hgrn_norm_w, rwkv_shift_mu=rwkv_shift_mu,
             rwkv_w0=rwkv_w0, rwkv_w2=rwkv_w2, rwkv_a0=rwkv_a0, rwkv_a2=rwkv_a2, rwkv_k_k=rwkv_k_k,
             rwkv_k_a=rwkv_k_a, rwkv_r_k=rwkv_r_k, rwkv_ln_w=rwkv_ln_w, rwkv_ln_b=rwkv_ln_b, s5_a_re=s5_a_re,
             s5_a_im=s5_a_im, s5_log_dt=s5_log_dt, s5_b_re=s5_b_re, s5_b_im=s5_b_im, s5_c_re=s5_c_re,
             s5_c_im=s5_c_im, s5_d=s5_d, s5_glu_w=s5_glu_w, s5_glu_b=s5_glu_b, w_branch=w_branch, w_gate=w_gate,
             b_gate=b_gate, w_out=w_out)
    layers = [_layer_params(l, a) for l in range(DEPTH)]
    meta = meta_tokens.astype(x_prompt.dtype)
    return (_trunk(x_prompt, meta, layers, final_norm_w), _trunk(x_sample, meta, layers, final_norm_w))
```
